```python
import math
import jax, jax.numpy as jnp
from jax import lax
import numpy as np

D_MODEL = 1024
BATCH = 4
SEQ = 8192
DEPTH = 2

N_META = 16
PREFIX = 128
BLOCK_Q = 128
N_EVEN = (DEPTH + 1) // 2
N_ODD = DEPTH // 2

POOL_WINDOWS = (2, 4, 8, 16)
N_POOL = len(POOL_WINDOWS)
POOL_GROUP = D_MODEL // 8
POOL_WIDTH = N_POOL * POOL_GROUP

DIFF_HEADS = 4
DIFF_QK_DIM = D_MODEL // 16
DIFF_V_DIM = 2 * DIFF_QK_DIM
DIFF_QK_WIDTH = DIFF_HEADS * 2 * DIFF_QK_DIM
DIFF_WIDTH = DIFF_HEADS * DIFF_V_DIM

EVEN_IN = POOL_WIDTH + 2 * DIFF_QK_WIDTH + DIFF_WIDTH
EVEN_MIX = POOL_WIDTH + DIFF_WIDTH

GLA_HEADS = 4
GLA_DK = D_MODEL // 2 // GLA_HEADS
GLA_DV = D_MODEL // GLA_HEADS
GLA_RANK = 16
GLA_TAU = 16.0
GLA_CHUNK = 64
ODD_IN = 2 * GLA_HEADS * GLA_DK + 2 * GLA_HEADS * GLA_DV + GLA_RANK

N_EXPERTS = 16
N_GROUPS = 4
EXPERTS_PER_GROUP = N_EXPERTS // N_GROUPS
TOP_K = 2
D_EXPERT = 512

ALPHA = (2.0 * DEPTH) ** 0.25
BETA = (8.0 * DEPTH) ** -0.25

kernel_name = 'hybrid_pool_diffattn_gla_groupmoe_deepnorm'


def layer_norm(x, g, b, eps=1e-5):
    xf = x.astype(jnp.float32)
    mu = jnp.mean(xf, -1, keepdims=True)
    var = jnp.mean(jnp.square(xf - mu), -1, keepdims=True)
    return ((xf - mu) * lax.rsqrt(var + eps) * g.astype(jnp.float32) + b.astype(jnp.float32)).astype(x.dtype)


def rms_norm(x, w, eps=1e-6):
    xf = x.astype(jnp.float32)
    return (xf * lax.rsqrt(jnp.mean(xf * xf, -1, keepdims=True) + eps) * w.astype(jnp.float32)).astype(x.dtype)


def pooling_mixer(u, valid, w_group, scale):
    B, L, _ = u.shape
    uf = u.astype(jnp.float32) * valid[None, :, None]
    csum = jnp.cumsum(uf, axis=1)
    ccnt = jnp.cumsum(valid)
    outs = []
    for gi, w in enumerate(POOL_WINDOWS):
        c = csum[..., gi * POOL_GROUP:(gi + 1) * POOL_GROUP]
        win_sum = c - jnp.pad(c, ((0, 0), (w, 0), (0, 0)))[:, :L]
        win_cnt = ccnt - jnp.pad(ccnt, (w, 0))[:L]
        mean = win_sum / jnp.maximum(win_cnt, 1.0)[None, :, None]
        outs.append(mean - uf[..., gi * POOL_GROUP:(gi + 1) * POOL_GROUP])
    d = jnp.stack(outs, axis=2)
    y = jnp.einsum('blgc,gcd->blgd', d, w_group.astype(jnp.float32)).reshape(B, L, POOL_WIDTH)
    return (y * scale.astype(jnp.float32)).astype(u.dtype)


def diff_attention(q, k, v, valid, lam, subln_w, lambda_init):
    B, L = q.shape[:2]
    nb = L // BLOCK_Q
    slopes = 2.0 ** (-8.0 * jnp.arange(1, DIFF_HEADS + 1, dtype=jnp.float32) / DIFF_HEADS)
    kpos = jnp.arange(L)
    key_ok = valid > 0
    scale = DIFF_QK_DIM ** -0.5
    qb = q.reshape(B, nb, BLOCK_Q, DIFF_HEADS, 2, DIFF_QK_DIM).swapaxes(0, 1)

    def block(args):
        qblk, bi = args
        qpos = bi * BLOCK_Q + jnp.arange(BLOCK_Q)
        s = jnp.einsum('bqhmd,bkhmd->bhmqk', qblk, k, preferred_element_type=jnp.float32) * scale
        dist = qpos[:, None] - kpos[None, :]
        allowed = (dist >= 0) & (key_ok[None, :] | (dist == 0))
        bias = -slopes[:, None, None] * dist.astype(jnp.float32)[None]
        s = jnp.where(allowed[None, None, None], s + bias[None, :, None], -jnp.inf)
        p = jax.nn.softmax(s, axis=-1)
        pdiff = p[:, :, 0] - lam * p[:, :, 1]
        return jnp.einsum('bhqk,bkhd->bqhd', pdiff.astype(v.dtype), v)

    o = lax.map(block, (qb, jnp.arange(nb)))
    o = o.swapaxes(0, 1).reshape(B, L, DIFF_HEADS, DIFF_V_DIM)
    o = rms_norm(o, subln_w) * (1.0 - lambda_init)
    return o.reshape(B, L, DIFF_WIDTH)


def gla_chunked(q, k, v, log_a):
    B, L, H, Dk = q.shape
    Dv = v.shape[-1]
    nc = L // GLA_CHUNK

    def to_chunks(t):
        return t.reshape(B, nc, GLA_CHUNK, H, t.shape[-1]).transpose(1, 0, 3, 2, 4)

    qc, kc, vc, ac = to_chunks(q), to_chunks(k), to_chunks(v), to_chunks(log_a)
    causal = jnp.tril(jnp.ones((GLA_CHUNK, GLA_CHUNK), dtype=bool))

    def step(S, inp):
        qi, ki, vi, ai = inp
        qf, kf, vf = qi.astype(jnp.float32), ki.astype(jnp.float32), vi.astype(jnp.float32)
        b = jnp.cumsum(ai.astype(jnp.float32), axis=2)
        inter = jnp.einsum('bhtk,bhkv->bhtv', qf * jnp.exp(b), S)
        rel = b[:, :, :, None, :] - b[:, :, None, :, :]
        decay = jnp.exp(jnp.where(causal[None, None, :, :, None], rel, -jnp.inf))
        att = jnp.einsum('bhtk,bhsk,bhtsk->bhts', qf, kf, decay)
        o = inter + jnp.einsum('bhts,bhsv->bhtv', att, vf)
        b_last = b[:, :, -1:, :]
        S = jnp.exp(b_last[:, :, 0, :, None]) * S + jnp.einsum('bhsk,bhsv->bhkv', kf * jnp.exp(b_last - b), vf)
        return S, o.astype(vi.dtype)

    S0 = jnp.zeros((B, H, Dk, Dv), jnp.float32)
    _, o = lax.scan(step, S0, (qc, kc, vc, ac))
    return o.transpose(1, 0, 3, 2, 4).reshape(B, L, H, Dv)


def even_mixer(h, valid, w_in, pool_w, pool_scale, lq1, lk1, lq2, lk2, subln, w_out, lambda_init):
    B, L, _ = h.shape
    z = h @ w_in
    u_pool, q, k, v = jnp.split(z, [POOL_WIDTH, POOL_WIDTH + DIFF_QK_WIDTH, POOL_WIDTH + 2 * DIFF_QK_WIDTH], axis=-1)
    y_pool = pooling_mixer(u_pool, valid, pool_w, pool_scale)
    lam = (jnp.exp(jnp.sum(lq1.astype(jnp.float32) * lk1.astype(jnp.float32)))
           - jnp.exp(jnp.sum(lq2.astype(jnp.float32) * lk2.astype(jnp.float32))) + lambda_init)
    y_attn = diff_attention(q.reshape(B, L, DIFF_HEADS, 2, DIFF_QK_DIM),
                            k.reshape(B, L, DIFF_HEADS, 2, DIFF_QK_DIM),
                            v.reshape(B, L, DIFF_HEADS, DIFF_V_DIM),
                            valid, lam, subln, lambda_init)
    return jnp.concatenate([y_pool, y_attn], axis=-1) @ w_out


def odd_mixer(h, valid, w_in, gate_w2, gate_b, head_norm, w_out):
    B, L, _ = h.shape
    nqk = GLA_HEADS * GLA_DK
    nv = GLA_HEADS * GLA_DV
    z = h @ w_in
    q, k, v, r, g_lr = jnp.split(z, [nqk, 2 * nqk, 2 * nqk + nv, 2 * nqk + 2 * nv], axis=-1)
    g = (g_lr @ gate_w2 + gate_b).astype(jnp.float32)
    log_a = (jax.nn.log_sigmoid(g) / GLA_TAU).reshape(B, L, GLA_HEADS, GLA_DK)
    q = q.reshape(B, L, GLA_HEADS, GLA_DK) * (GLA_DK ** -0.5)
    k = k.reshape(B, L, GLA_HEADS, GLA_DK) * valid.astype(k.dtype)[None, :, None, None]
    o = gla_chunked(q, k, v.reshape(B, L, GLA_HEADS, GLA_DV), log_a)
    o = rms_norm(o, head_norm) * jax.nn.silu(r).reshape(B, L, GLA_HEADS, GLA_DV)
    return o.reshape(B, L, nv) @ w_out


def grouped_moe(h, router_w, router_bias, w_gate, w_up, w_down):
    B, L, D = h.shape
    xt = h.reshape(-1, D)
    probs = jax.nn.softmax(jnp.dot(xt, router_w, preferred_element_type=jnp.float32), axis=-1)
    sel = (probs + router_bias.astype(jnp.float32)).reshape(-1, N_GROUPS, EXPERTS_PER_GROUP)
    group_score = lax.top_k(sel, TOP_K)[0].sum(-1)
    g = jnp.argmax(group_score, axis=-1)
    sel_g = jnp.take_along_axis(sel, g[:, None, None], axis=1)[:, 0]
    _, local = lax.top_k(sel_g, TOP_K)
    idx = g[:, None] * EXPERTS_PER_GROUP + local
    w = jnp.take_along_axis(probs, idx, axis=1)
    w = w / jnp.sum(w, -1, keepdims=True)
    combine = jnp.sum(jax.nn.one_hot(idx, N_EXPERTS, dtype=jnp.float32) * w[..., None], axis=1)
    y = jnp.zeros(xt.shape, jnp.float32)
    for e in range(N_EXPERTS):
        act = jax.nn.silu(xt @ w_gate[e]) * (xt @ w_up[e])
        y = y + combine[:, e:e + 1] * (act @ w_down[e]).astype(jnp.float32)
    return y.astype(h.dtype).reshape(B, L, D)


def setup_inputs(seed: int = 0) -> dict:
    key = jax.random.key(seed)
    ks = jax.random.split(key, 32)

    def nrm(i, shape, scale):
        return jax.random.normal(ks[i], shape, jnp.float32) * scale

    ne, no = N_EVEN, N_ODD
    return {
        'x': nrm(0, (BATCH, SEQ, D_MODEL), 1.0),
        'meta': nrm(1, (N_META, D_MODEL), 1.0),
        'even_w_in': nrm(2, (ne, D_MODEL, EVEN_IN), D_MODEL ** -0.5),
        'pool_w': nrm(3, (ne, N_POOL, POOL_GROUP, POOL_GROUP), POOL_GROUP ** -0.5),
        'pool_scale': 1.0 + nrm(4, (ne, POOL_WIDTH), 0.02),
        'diff_lq1': nrm(5, (ne, DIFF_QK_DIM), 0.1),
        'diff_lk1': nrm(6, (ne, DIFF_QK_DIM), 0.1),
        'diff_lq2': nrm(7, (ne, DIFF_QK_DIM), 0.1),
        'diff_lk2': nrm(8, (ne, DIFF_QK_DIM), 0.1),
        'diff_subln': 1.0 + nrm(9, (ne, DIFF_V_DIM), 0.02),
        'even_w_out': nrm(10, (ne, EVEN_MIX, D_MODEL), EVEN_MIX ** -0.5 * BETA),
        'odd_w_in': nrm(11, (no, D_MODEL, ODD_IN), D_MODEL ** -0.5),
        'gla_gate_w2': nrm(12, (no, GLA_RANK, GLA_HEADS * GLA_DK), GLA_RANK ** -0.5),
        'gla_gate_b': nrm(13, (no, GLA_HEADS * GLA_DK), 0.1),
        'gla_head_norm': 1.0 + nrm(14, (no, GLA_DV), 0.02),
        'odd_w_out': nrm(15, (no, GLA_HEADS * GLA_DV, D_MODEL), (GLA_HEADS * GLA_DV) ** -0.5 * BETA),
        'ln_mix_g': 1.0 + nrm(16, (DEPTH, D_MODEL), 0.02),
        'ln_mix_b': nrm(17, (DEPTH, D_MODEL), 0.02),
        'ln_ffn_g': 1.0 + nrm(18, (DEPTH, D_MODEL), 0.02),
        'ln_ffn_b': nrm(19, (DEPTH, D_MODEL), 0.02),
        'router_w': nrm(20, (D_MODEL, N_EXPERTS), D_MODEL ** -0.5),
        'router_bias': nrm(21, (N_EXPERTS,), 0.01),
        'moe_w_gate': nrm(22, (DEPTH, N_EXPERTS, D_MODEL, D_EXPERT), D_MODEL ** -0.5),
        'moe_w_up': nrm(23, (DEPTH, N_EXPERTS, D_MODEL, D_EXPERT), D_MODEL ** -0.5),
        'moe_w_down': nrm(24, (DEPTH, N_EXPERTS, D_EXPERT, D_MODEL), D_EXPERT ** -0.5 * BETA),
    }


def reference(x, meta, even_w_in, pool_w, pool_scale, diff_lq1, diff_lk1, diff_lq2, diff_lk2,
              diff_subln, even_w_out, odd_w_in, gla_gate_w2, gla_gate_b, gla_head_norm, odd_w_out,
              ln_mix_g, ln_mix_b, ln_ffn_g, ln_ffn_b, router_w, router_bias,
              moe_w_gate, moe_w_up, moe_w_down):
    B = x.shape[0]
    L = PREFIX + x.shape[1]
    pad = jnp.zeros((B, PREFIX - N_META, D_MODEL), x.dtype)
    metas = jnp.broadcast_to(meta.astype(x.dtype)[None], (B, N_META, D_MODEL))
    h = jnp.concatenate([pad, metas, x], axis=1)
    valid = (jnp.arange(L) >= PREFIX - N_META).astype(jnp.float32)
    for i in range(DEPTH):
        j = i // 2
        if i % 2 == 0:
            lambda_init = 0.8 - 0.6 * math.exp(-0.3 * i)
            mix = even_mixer(h, valid, even_w_in[j], pool_w[j], pool_scale[j], diff_lq1[j], diff_lk1[j],
                             diff_lq2[j], diff_lk2[j], diff_subln[j], even_w_out[j], lambda_init)
        else:
            mix = odd_mixer(h, valid, odd_w_in[j], gla_gate_w2[j], gla_gate_b[j], gla_head_norm[j], odd_w_out[j])
        h = layer_norm(ALPHA * h + mix, ln_mix_g[i], ln_mix_b[i])
        ffn = grouped_moe(h, router_w, router_bias, moe_w_gate[i], moe_w_up[i], moe_w_down[i])
        h = layer_norm(ALPHA * h + ffn, ln_ffn_g[i], ln_ffn_b[i])
    return h[:, PREFIX:]
```

```python
import functools

import jax
import jax.numpy as jnp
from jax import lax
from jax.experimental import pallas as pl
from jax.experimental.pallas import tpu as pltpu

F32 = jnp.float32
BF16 = jnp.bfloat16

D_MODEL = 1024
DEPTH = 2
N_META = 16
PREFIX = 128
FIRST_VALID = PREFIX - N_META
POOL_WINDOWS = (2, 4, 8, 16)
POOL_GROUP = 128
POOL_WIDTH = 512
MAX_WINDOW = 16
DIFF_HEADS = 4
DIFF_QK_DIM = 64
DIFF_V_DIM = 128
DIFF_WIDTH = 512
EVEN_IN = 2048
GLA_HEADS = 4
GLA_DK = 128
GLA_DV = 256
GLA_RANK = 16
GLA_TAU = 16.0
GLA_QK_WIDTH = GLA_HEADS * GLA_DK
GLA_V_WIDTH = GLA_HEADS * GLA_DV
ODD_IN = 2 * GLA_QK_WIDTH + 2 * GLA_V_WIDTH + GLA_RANK
ODD_IN_PAD = 3200
N_EXPERTS = 16
N_GROUPS = 4
EXPERTS_PER_GROUP = 4
D_EXPERT = 512
ALPHA = (2.0 * DEPTH) ** 0.25
NEG_BIG = -1e30

LANES = 128
VMEM_LIMIT = 56 * 1024 * 1024


def _pick(n, candidates):
    for c in candidates:
        if n % c == 0:
            return c
    raise ValueError(f"no tile for {n} in {candidates}")


def _cparams(sem):
    return pltpu.CompilerParams(dimension_semantics=sem, vmem_limit_bytes=VMEM_LIMIT)


def _layer_norm(a, g, b):
    mu = jnp.mean(a, axis=-1, keepdims=True)
    d = a - mu
    var = jnp.mean(d * d, axis=-1, keepdims=True)
    return d * lax.rsqrt(var + 1e-5) * g + b


def _proj_kernel(x_ref, w_ref, o_ref, *, tn):
    xb = x_ref[...].astype(BF16)
    for j in range(o_ref.shape[1] // tn):
        cols = slice(j * tn, (j + 1) * tn)
        o_ref[:, cols] = jnp.dot(xb, w_ref[:, cols], preferred_element_type=F32).astype(o_ref.dtype)


def _proj(x2d, w, name):
    n, k = x2d.shape
    wout = w.shape[1]
    tm = _pick(n, (640, 512, 256, 128))
    tn = _pick(wout, (640, 512, 128))
    return pl.pallas_call(
        functools.partial(_proj_kernel, tn=tn),
        grid=(n // tm,),
        in_specs=[pl.BlockSpec((tm, k), lambda i: (i, 0)),
                  pl.BlockSpec((k, wout), lambda i: (0, 0))],
        out_specs=pl.BlockSpec((tm, wout), lambda i: (i, 0)),
        out_shape=jax.ShapeDtypeStruct((n, wout), BF16),
        compiler_params=_cparams(("parallel",)),
        name=name,
    )(x2d, w)


def _pool_kernel(cur_ref, halo_ref, w_ref, sc_ref, o_ref, u_scr):
    t = pl.program_id(1)
    tl = cur_ref.shape[0]
    pos = t * tl + lax.broadcasted_iota(jnp.int32, (tl, 1), 0)
    hpos = t * tl - MAX_WINDOW + lax.broadcasted_iota(jnp.int32, (MAX_WINDOW, 1), 0)
    u_scr[0:MAX_WINDOW, :] = jnp.where(hpos >= FIRST_VALID, halo_ref[...].astype(F32), 0.0)
    u_scr[MAX_WINDOW:, :] = jnp.where(pos >= FIRST_VALID, cur_ref[...].astype(F32), 0.0)
    n_valid = pos - (FIRST_VALID - 1)
    for gi, w in enumerate(POOL_WINDOWS):
        cols = slice(gi * POOL_GROUP, (gi + 1) * POOL_GROUP)
        u = u_scr[MAX_WINDOW:MAX_WINDOW + tl, cols]
        win = u
        for j in range(1, w):
            win = win + u_scr[MAX_WINDOW - j:MAX_WINDOW - j + tl, cols]
        cnt = jnp.clip(n_valid, 1, w).astype(F32)
        d = win / cnt - u
        y = jnp.dot(d.astype(BF16), w_ref[gi], preferred_element_type=F32) * sc_ref[:, cols]
        o_ref[:, cols] = y.astype(o_ref.dtype)


def _pool(z, pool_w, pool_scale):
    b, l, _ = z.shape
    tl = _pick(l, (640, 128))
    hb = tl // MAX_WINDOW
    return pl.pallas_call(
        _pool_kernel,
        grid=(b, l // tl),
        in_specs=[pl.BlockSpec((None, tl, POOL_WIDTH), lambda bi, t: (bi, t, 0)),
                  pl.BlockSpec((None, MAX_WINDOW, POOL_WIDTH),
                               lambda bi, t: (bi, jnp.maximum(t * hb - 1, 0), 0)),
                  pl.BlockSpec((len(POOL_WINDOWS), POOL_GROUP, POOL_GROUP), lambda bi, t: (0, 0, 0)),
                  pl.BlockSpec((1, POOL_WIDTH), lambda bi, t: (0, 0))],
        out_specs=pl.BlockSpec((None, tl, POOL_WIDTH), lambda bi, t: (bi, t, 0)),
        out_shape=jax.ShapeDtypeStruct((b, l, POOL_WIDTH), BF16),
        scratch_shapes=[pltpu.VMEM((tl + MAX_WINDOW, POOL_WIDTH), F32)],
        compiler_params=_cparams(("parallel", "parallel")),
        name="pool_mixer",
    )(z, z, pool_w, pool_scale)


def _attn_kernel(lam_ref, q_ref, k_ref, v_ref, sub_ref, o_ref, m_scr, l_scr, acc_scr, nb_scr,
                 *, tq, lambda_init):
    h = pl.program_id(1)
    i = pl.program_id(2)
    slope = jnp.exp2(-2.0 * (h + 1).astype(F32))
    lane = lax.broadcasted_iota(jnp.int32, (tq, 2 * DIFF_QK_DIM), 1)
    q = q_ref[...] * jnp.asarray(DIFF_QK_DIM ** -0.5, BF16)
    zero = jnp.zeros_like(q)
    q_maps = (jnp.where(lane < DIFF_QK_DIM, q, zero), jnp.where(lane >= DIFF_QK_DIM, q, zero))
    rc = (lax.broadcasted_iota(jnp.int32, (tq, tq), 0)
          - lax.broadcasted_iota(jnp.int32, (tq, tq), 1))
    nb_scr[...] = -slope * rc.astype(F32)
    m_scr[...] = jnp.full(m_scr.shape, NEG_BIG, F32)
    l_scr[...] = jnp.zeros(l_scr.shape, F32)
    acc_scr[...] = jnp.zeros(acc_scr.shape, F32)

    def tile(j, causal, keymask):
        start = pl.multiple_of(j * tq, tq)
        kt = k_ref[pl.ds(start, tq), :]
        vt = v_ref[pl.ds(start, tq), :]
        bias = nb_scr[...] - slope * ((i - j) * tq).astype(F32)
        allowed = None
        if causal:
            allowed = rc >= 0
        if keymask:
            kcol = lax.broadcasted_iota(jnp.int32, (tq, tq), 1)
            ok = kcol >= FIRST_VALID
            if causal:
                ok = jnp.logical_or(ok, rc == 0)
                allowed = jnp.logical_and(allowed, ok)
            else:
                allowed = ok
        for mp in range(2):
            s = lax.dot_general(q_maps[mp], kt, (((1,), (1,)), ((), ())),
                                preferred_element_type=F32) + bias
            if allowed is not None:
                s = jnp.where(allowed, s, NEG_BIG)
            m_prev = m_scr[mp]
            m_new = jnp.maximum(m_prev, jnp.max(s, axis=-1, keepdims=True))
            alpha = jnp.exp(m_prev - m_new)
            p = jnp.exp(s - m_new)
            l_scr[mp] = alpha * l_scr[mp] + jnp.sum(p, axis=-1, keepdims=True)
            acc_scr[mp] = alpha * acc_scr[mp] + jnp.dot(p.astype(BF16), vt, preferred_element_type=F32)
            m_scr[mp] = m_new

    @pl.when(i == 0)
    def _():
        tile(0, True, True)

    @pl.when(i > 0)
    def _():
        tile(0, False, True)
        lax.fori_loop(1, i, lambda j, c: (tile(j, False, False), c)[1], 0)
        tile(i, True, False)

    lam = (jnp.exp(jnp.sum(lam_ref[0:1, :] * lam_ref[1:2, :], axis=-1, keepdims=True))
           - jnp.exp(jnp.sum(lam_ref[2:3, :] * lam_ref[3:4, :], axis=-1, keepdims=True)) + lambda_init)
    o = acc_scr[0] / l_scr[0] - lam * (acc_scr[1] / l_scr[1])
    o = o * lax.rsqrt(jnp.mean(o * o, axis=-1, keepdims=True) + 1e-6) * sub_ref[...] * (1.0 - lambda_init)
    o_ref[...] = o.astype(o_ref.dtype)


def _attn(z, lam_params, subln, lambda_init):
    b, l, _ = z.shape
    tq = _pick(l, (640, 128))
    qb, kb, vb = POOL_WIDTH // LANES, (POOL_WIDTH + DIFF_WIDTH) // LANES, (POOL_WIDTH + 2 * DIFF_WIDTH) // LANES
    return pl.pallas_call(
        functools.partial(_attn_kernel, tq=tq, lambda_init=lambda_init),
        grid=(b, DIFF_HEADS, l // tq),
        in_specs=[pl.BlockSpec((4, DIFF_QK_DIM), lambda bi, h, i: (0, 0)),
                  pl.BlockSpec((None, tq, LANES), lambda bi, h, i: (bi, i, qb + h)),
                  pl.BlockSpec((None, l, LANES), lambda bi, h, i: (bi, 0, kb + h)),
                  pl.BlockSpec((None, l, LANES), lambda bi, h, i: (bi, 0, vb + h)),
                  pl.BlockSpec((1, DIFF_V_DIM), lambda bi, h, i: (0, 0))],
        out_specs=pl.BlockSpec((None, tq, DIFF_V_DIM), lambda bi, h, i: (bi, i, h)),
        out_shape=jax.ShapeDtypeStruct((b, l, DIFF_WIDTH), BF16),
        scratch_shapes=[pltpu.VMEM((2, tq, 1), F32), pltpu.VMEM((2, tq, 1), F32),
                        pltpu.VMEM((2, tq, DIFF_V_DIM), F32), pltpu.VMEM((tq, tq), F32)],
        compiler_params=_cparams(("parallel", "parallel", "arbitrary")),
        name="diff_attn",
    )(lam_params, z, z, z, subln)


def _gla_kernel(q_ref, k_ref, v_ref, r_ref, glr_ref, w2_ref, gb_ref, hn_ref, o_ref, st_ref, *, c_len, sb):
    c = pl.program_id(1)

    @pl.when(c == 0)
    def _():
        st_ref[...] = jnp.zeros(st_ref.shape, F32)

    pos = c * c_len + lax.broadcasted_iota(jnp.int32, (c_len, 1), 0)
    validf = (pos >= FIRST_VALID).astype(F32)
    ri = lax.broadcasted_iota(jnp.int32, (c_len, c_len), 0)
    ci = lax.broadcasted_iota(jnp.int32, (c_len, c_len), 1)
    tri = jnp.where(ri >= ci, 1.0, 0.0).astype(BF16)
    row = lax.broadcasted_iota(jnp.int32, (c_len, 1), 0)
    t_loc = lax.broadcasted_iota(jnp.int32, (sb, 1), 0)
    lane_c = lax.broadcasted_iota(jnp.int32, (sb, c_len), 1)
    glr = glr_ref[...]
    nt = (((1,), (1,)), ((), ()))
    tn = (((0,), (0,)), ((), ()))

    for h in range(GLA_HEADS):
        ks = slice(h * GLA_DK, (h + 1) * GLA_DK)
        vs = slice(h * GLA_DV, (h + 1) * GLA_DV)
        g = jnp.dot(glr, w2_ref[:, ks], preferred_element_type=F32) + gb_ref[:, ks]
        log_a = -(jnp.maximum(-g, 0.0) + jnp.log(1.0 + jnp.exp(-jnp.abs(g)))) * (1.0 / GLA_TAU)
        la_hi = log_a.astype(BF16)
        la_lo = (log_a - la_hi.astype(F32)).astype(BF16)
        b = (jnp.dot(tri, la_hi, preferred_element_type=F32)
             + jnp.dot(tri, la_lo, preferred_element_type=F32))
        q = q_ref[:, ks].astype(F32) * (GLA_DK ** -0.5)
        k = k_ref[:, ks].astype(F32) * validf
        v = v_ref[:, vs]
        st = st_ref[h]

        blocks = []
        for i in range(c_len // sb):
            lo = i * sb
            q_i = q[lo:lo + sb]
            b_i = b[lo:lo + sb]
            a_i = jnp.zeros((sb, c_len), F32)
            if i > 0:
                b_ref_row = b[lo - 1:lo]
                q_t = q_i * jnp.exp(b_i - b_ref_row)
                k_t = jnp.where(row < lo, k * jnp.exp(jnp.minimum(b_ref_row - b, 0.0)), 0.0)
                a_i = lax.dot_general(q_t.astype(BF16), k_t.astype(BF16), nt, preferred_element_type=F32)
            for s in range(sb):
                r = lo + s
                e = jnp.exp(jnp.minimum(b_i - b[r:r + 1], 0.0))
                col = jnp.sum(q_i * k[r:r + 1] * e, axis=-1, keepdims=True)
                col = jnp.where(t_loc >= s, col, 0.0)
                a_i = jnp.where(lane_c == r, col, a_i)
            blocks.append(a_i)
        att = jnp.concatenate(blocks, axis=0)

        o = jnp.dot(att.astype(BF16), v, preferred_element_type=F32)
        o = o + lax.dot_general((q * jnp.exp(b)).astype(BF16), st.astype(BF16), nt,
                                preferred_element_type=F32)
        b_last = b[c_len - 1:c_len]
        k_hat = (k * jnp.exp(b_last - b)).astype(BF16)
        st_ref[h] = st * jnp.exp(b_last) + lax.dot_general(v, k_hat, tn, preferred_element_type=F32)

        o = o * lax.rsqrt(jnp.mean(o * o, axis=-1, keepdims=True) + 1e-6) * hn_ref[...]
        rg = r_ref[:, vs].astype(F32)
        o_ref[:, vs] = (o * (rg * jax.nn.sigmoid(rg))).astype(o_ref.dtype)


def _gla(z, w2, gate_b, head_norm):
    b, l, _ = z.shape
    c_len = 64
    qw, vw = GLA_QK_WIDTH, GLA_V_WIDTH
    return pl.pallas_call(
        functools.partial(_gla_kernel, c_len=c_len, sb=16),
        grid=(b, l // c_len),
        in_specs=[pl.BlockSpec((None, c_len, qw), lambda bi, c: (bi, c, 0)),
                  pl.BlockSpec((None, c_len, qw), lambda bi, c: (bi, c, 1)),
                  pl.BlockSpec((None, c_len, vw), lambda bi, c: (bi, c, 1)),
                  pl.BlockSpec((None, c_len, vw), lambda bi, c: (bi, c, 2)),
                  pl.BlockSpec((None, c_len, LANES), lambda bi, c: (bi, c, (2 * qw + 2 * vw) // LANES)),
                  pl.BlockSpec((LANES, qw), lambda bi, c: (0, 0)),
                  pl.BlockSpec((1, qw), lambda bi, c: (0, 0)),
                  pl.BlockSpec((1, GLA_DV), lambda bi, c: (0, 0))],
        out_specs=pl.BlockSpec((None, c_len, vw), lambda bi, c: (bi, c, 0)),
        out_shape=jax.ShapeDtypeStruct((b, l, vw), BF16),
        scratch_shapes=[pltpu.VMEM((GLA_HEADS, GLA_DV, GLA_DK), F32)],
        compiler_params=_cparams(("parallel", "arbitrary")),
        name="gla_mixer",
    )(z, z, z, z, z, w2, gate_b, head_norm)


def _mixout_kernel(*refs):
    *y_refs, w_ref, h_ref, g_ref, b_ref, o_ref = refs
    acc = ALPHA * h_ref[...]
    off = 0
    for y_ref in y_refs:
        kd = y_ref.shape[1]
        acc = acc + jnp.dot(y_ref[...], w_ref[off:off + kd, :], preferred_element_type=F32)
        off += kd
    o_ref[...] = _layer_norm(acc, g_ref[...], b_ref[...])


def _mixout(ys, w, h2d, g, b, name):
    n, d = h2d.shape
    tm = _pick(n, (640, 512, 256, 128))
    return pl.pallas_call(
        _mixout_kernel,
        grid=(n // tm,),
        in_specs=[pl.BlockSpec((tm, y.shape[1]), lambda i: (i, 0)) for y in ys]
        + [pl.BlockSpec(w.shape, lambda i: (0, 0)),
           pl.BlockSpec((tm, d), lambda i: (i, 0)),
           pl.BlockSpec((1, d), lambda i: (0, 0)),
           pl.BlockSpec((1, d), lambda i: (0, 0))],
        out_specs=pl.BlockSpec((tm, d), lambda i: (i, 0)),
        out_shape=jax.ShapeDtypeStruct((n, d), F32),
        compiler_params=_cparams(("parallel",)),
        name=name,
    )(*ys, w, h2d, g, b)


def _router_kernel(bias_ref, h_ref, rw_ref, comb_ref):
    tm = h_ref.shape[0]
    x = h_ref[...]
    w = rw_ref[...]
    xh = x.astype(BF16)
    xl = (x - xh.astype(F32)).astype(BF16)
    wh = w.astype(BF16)
    wl = (w - wh.astype(F32)).astype(BF16)
    nt = (((1,), (1,)), ((), ()))
    logits = (lax.dot_general(wh, xh, nt, preferred_element_type=F32)
              + lax.dot_general(wh, xl, nt, preferred_element_type=F32)
              + lax.dot_general(wl, xh, nt, preferred_element_type=F32))
    lg = [logits[e:e + 1, :] for e in range(N_EXPERTS)]
    mx = functools.reduce(jnp.maximum, lg)
    ex = [jnp.exp(v - mx) for v in lg]
    den = functools.reduce(jnp.add, ex)
    probs = [v / den for v in ex]
    sel = [probs[e] + bias_ref[e] for e in range(N_EXPERTS)]

    def top2_sum(a, b, c, d):
        hi1, lo1 = jnp.maximum(a, b), jnp.minimum(a, b)
        hi2, lo2 = jnp.maximum(c, d), jnp.minimum(c, d)
        return jnp.maximum(hi1, hi2) + jnp.maximum(jnp.minimum(hi1, hi2), jnp.maximum(lo1, lo2))

    best = top2_sum(*sel[0:4])
    gidx = jnp.zeros((1, tm), jnp.int32)
    for g in range(1, N_GROUPS):
        sc = top2_sum(*sel[4 * g:4 * g + 4])
        better = sc > best
        gidx = jnp.where(better, g, gidx)
        best = jnp.maximum(best, sc)

    def pick(vals, j):
        out = vals[j]
        for g in range(1, N_GROUPS):
            out = jnp.where(gidx == g, vals[4 * g + j], out)
        return out

    sg = [pick(sel, j) for j in range(EXPERTS_PER_GROUP)]
    pg = [pick(probs, j) for j in range(EXPERTS_PER_GROUP)]
    chosen = []
    for j in range(EXPERTS_PER_GROUP):
        rank = jnp.zeros((1, tm), jnp.int32)
        for o in range(EXPERTS_PER_GROUP):
            if o == j:
                continue
            ahead = (sg[o] > sg[j]) if o > j else (sg[o] >= sg[j])
            rank = rank + ahead.astype(jnp.int32)
        chosen.append(jnp.where(rank < 2, pg[j], 0.0))
    wsum = functools.reduce(jnp.add, chosen)
    wn = [cj / wsum for cj in chosen]
    rows = []
    for e in range(N_EXPERTS):
        rows.append(jnp.where(gidx == e // EXPERTS_PER_GROUP, wn[e % EXPERTS_PER_GROUP], 0.0))
    rows.append(jnp.zeros((LANES - N_EXPERTS, tm), F32))
    comb_t = jnp.concatenate(rows, axis=0)
    comb_ref[...] = comb_t.T


def _router(h2d, router_w_t, router_bias):
    n, d = h2d.shape
    tm = _pick(n, (640, 512, 256, 128))
    return pl.pallas_call(
        _router_kernel,
        grid=(n // tm,),
        in_specs=[pl.BlockSpec(memory_space=pltpu.SMEM),
                  pl.BlockSpec((tm, d), lambda i: (i, 0)),
                  pl.BlockSpec((N_EXPERTS, d), lambda i: (0, 0))],
        out_specs=pl.BlockSpec((tm, LANES), lambda i: (i, 0)),
        out_shape=jax.ShapeDtypeStruct((n, LANES), F32),
        compiler_params=_cparams(("parallel",)),
        name="router",
    )(router_bias, h2d, router_w_t)


def _moe_kernel(h_ref, comb_ref, wg_ref, wu_ref, wd_ref, g_ref, b_ref, o_ref, xb_scr, acc_scr):
    e = pl.program_id(1)

    @pl.when(e == 0)
    def _():
        xb_scr[...] = h_ref[...].astype(BF16)
        acc_scr[...] = jnp.zeros(acc_scr.shape, F32)

    xb = xb_scr[...]
    gate = jnp.dot(xb, wg_ref[...], preferred_element_type=F32)
    up = jnp.dot(xb, wu_ref[...], preferred_element_type=F32)
    lane = lax.broadcasted_iota(jnp.int32, comb_ref.shape, 1)
    ce = jnp.sum(jnp.where(lane == e, comb_ref[...], 0.0), axis=-1, keepdims=True)
    act = gate * jax.nn.sigmoid(gate) * up * ce
    acc_scr[...] += jnp.dot(act.astype(BF16), wd_ref[...], preferred_element_type=F32)

    @pl.when(e == N_EXPERTS - 1)
    def _():
        o_ref[...] = _layer_norm(ALPHA * h_ref[...] + acc_scr[...], g_ref[...], b_ref[...])


def _moe(h2d, comb, wg, wu, wd, g, b, name):
    n, d = h2d.shape
    tm = _pick(n, (1280, 640, 512, 256, 128))
    return pl.pallas_call(
        _moe_kernel,
        grid=(n // tm, N_EXPERTS),
        in_specs=[pl.BlockSpec((tm, d), lambda i, e: (i, 0)),
                  pl.BlockSpec((tm, LANES), lambda i, e: (i, 0)),
                  pl.BlockSpec((None, d, D_EXPERT), lambda i, e: (e, 0, 0)),
                  pl.BlockSpec((None, d, D_EXPERT), lambda i, e: (e, 0, 0)),
                  pl.BlockSpec((None, D_EXPERT, d), lambda i, e: (e, 0, 0)),
                  pl.BlockSpec((1, d), lambda i, e: (0, 0)),
                  pl.BlockSpec((1, d), lambda i, e: (0, 0))],
        out_specs=pl.BlockSpec((tm, d), lambda i, e: (i, 0)),
        out_shape=jax.ShapeDtypeStruct((n, d), F32),
        scratch_shapes=[pltpu.VMEM((tm, d), BF16), pltpu.VMEM((tm, d), F32)],
        compiler_params=_cparams(("parallel", "arbitrary")),
        name=name,
    )(h2d, comb, wg, wu, wd, g, b)


def kernel(x, meta, even_w_in, pool_w, pool_scale, diff_lq1, diff_lk1, diff_lq2, diff_lk2, diff_subln,
           even_w_out, odd_w_in, gla_gate_w2, gla_gate_b, gla_head_norm, odd_w_out, ln_mix_g, ln_mix_b,
           ln_ffn_g, ln_ffn_b, router_w, router_bias, moe_w_gate, moe_w_up, moe_w_down):
    import math
    bsz, seq, d = x.shape
    l = PREFIX + seq
    n = bsz * l
    pad = jnp.zeros((bsz, FIRST_VALID, d), x.dtype)
    metas = jnp.broadcast_to(meta.astype(x.dtype)[None], (bsz, N_META, d))
    h = jnp.concatenate([pad, metas, x], axis=1).reshape(n, d)
    router_w_t = router_w.T
    row = lambda a: a.reshape(1, -1)

    for i in range(DEPTH):
        j = i // 2
        if i % 2 == 0:
            lambda_init = 0.8 - 0.6 * math.exp(-0.3 * i)
            z = _proj(h, even_w_in[j].astype(BF16), "even_in_proj").reshape(bsz, l, EVEN_IN)
            y_pool = _pool(z, pool_w[j].astype(BF16), row(pool_scale[j]))
            lam_params = jnp.stack([diff_lq1[j], diff_lk1[j], diff_lq2[j], diff_lk2[j]])
            y_attn = _attn(z, lam_params, row(diff_subln[j]), lambda_init)
            ys = [y_pool.reshape(n, POOL_WIDTH), y_attn.reshape(n, DIFF_WIDTH)]
            h = _mixout(ys, even_w_out[j].astype(BF16), h, row(ln_mix_g[i]), row(ln_mix_b[i]), "even_out_proj")
        else:
            w_in = jnp.pad(odd_w_in[j], ((0, 0), (0, ODD_IN_PAD - ODD_IN))).astype(BF16)
            z = _proj(h, w_in, "odd_in_proj").reshape(bsz, l, ODD_IN_PAD)
            w2 = jnp.pad(gla_gate_w2[j], ((0, LANES - GLA_RANK), (0, 0))).astype(BF16)
            y = _gla(z, w2, row(gla_gate_b[j]), row(gla_head_norm[j]))
            h = _mixout([y.reshape(n, GLA_V_WIDTH)], odd_w_out[j].astype(BF16), h,
                        row(ln_mix_g[i]), row(ln_mix_b[i]), "odd_out_proj")
        comb = _router(h, router_w_t, router_bias)
        h = _moe(h, comb, moe_w_gate[i].astype(BF16), moe_w_up[i].astype(BF16), moe_w_down[i].astype(BF16),
                 row(ln_ffn_g[i]), row(ln_ffn_b[i]), f"moe_{i}")
    return h.reshape(bsz, l, d)[:, PREFIX:]
```

```python
import functools
import math

import jax
import jax.numpy as jnp
from jax import lax
from jax.experimental import pallas as pl
from jax.experimental.pallas import tpu as pltpu

F32 = jnp.float32
BF16 = jnp.bfloat16

D_MODEL = 1024
DEPTH = 2
N_META = 16
PREFIX = 128
FIRST_VALID = PREFIX - N_META
POOL_WINDOWS = (2, 4, 8, 16)
POOL_GROUP = 128
POOL_WIDTH = 512
MAX_WINDOW = 16
DIFF_HEADS = 4
DIFF_QK_DIM = 64
DIFF_V_DIM = 128
DIFF_WIDTH = 512
EVEN_IN = 2048
GLA_HEADS = 4
GLA_DK = 128
GLA_DV = 256
GLA_RANK = 16
GLA_TAU = 16.0
GLA_QK_WIDTH = GLA_HEADS * GLA_DK
GLA_V_WIDTH = GLA_HEADS * GLA_DV
ODD_IN = 2 * GLA_QK_WIDTH + 2 * GLA_V_WIDTH + GLA_RANK
ODD_IN_PAD = 3200
N_EXPERTS = 16
N_GROUPS = 4
EXPERTS_PER_GROUP = 4
D_EXPERT = 512
ALPHA = (2.0 * DEPTH) ** 0.25
NEG_BIG = -1e30

LANES = 128
VMEM_LIMIT = 56 * 1024 * 1024


def _pick(n, candidates):
    for c in candidates:
        if n % c == 0:
            return c
    raise ValueError(f"no tile for {n} in {candidates}")


def _cparams(sem):
    return pltpu.CompilerParams(dimension_semantics=sem, vmem_limit_bytes=VMEM_LIMIT)


def _layer_norm(a, g, b):
    mu = jnp.mean(a, axis=-1, keepdims=True)
    d = a - mu
    var = jnp.mean(d * d, axis=-1, keepdims=True)
    return d * lax.rsqrt(var + 1e-5) * g + b


def _proj_kernel(x_ref, w_ref, o_ref, *, tn):
    xb = x_ref[...].astype(BF16)
    for j in range(o_ref.shape[1] // tn):
        cols = slice(j * tn, (j + 1) * tn)
        o_ref[:, cols] = jnp.dot(xb, w_ref[:, cols], preferred_element_type=F32).astype(o_ref.dtype)


def _proj(x2d, w, name):
    n, k = x2d.shape
    wout = w.shape[1]
    tm = _pick(n, (640, 512, 256, 128))
    tn = _pick(wout, (640, 512, 128))
    return pl.pallas_call(
        functools.partial(_proj_kernel, tn=tn),
        grid=(n // tm,),
        in_specs=[pl.BlockSpec((tm, k), lambda i: (i, 0)),
                  pl.BlockSpec((k, wout), lambda i: (0, 0))],
        out_specs=pl.BlockSpec((tm, wout), lambda i: (i, 0)),
        out_shape=jax.ShapeDtypeStruct((n, wout), BF16),
        compiler_params=_cparams(("parallel",)),
        name=name,
    )(x2d, w)


def _pool_kernel(cur_ref, halo_ref, w_ref, sc_ref, o_ref, u_scr):
    t = pl.program_id(1)
    tl = cur_ref.shape[0]
    pos = t * tl + lax.broadcasted_iota(jnp.int32, (tl, 1), 0)
    hpos = t * tl - MAX_WINDOW + lax.broadcasted_iota(jnp.int32, (MAX_WINDOW, 1), 0)
    u_scr[0:MAX_WINDOW, :] = jnp.where(hpos >= FIRST_VALID, halo_ref[...].astype(F32), 0.0)
    u_scr[MAX_WINDOW:, :] = jnp.where(pos >= FIRST_VALID, cur_ref[...].astype(F32), 0.0)
    n_valid = pos - (FIRST_VALID - 1)
    for gi, w in enumerate(POOL_WINDOWS):
        cols = slice(gi * POOL_GROUP, (gi + 1) * POOL_GROUP)
        u = u_scr[MAX_WINDOW:MAX_WINDOW + tl, cols]
        win = u
        for j in range(1, w):
            win = win + u_scr[MAX_WINDOW - j:MAX_WINDOW - j + tl, cols]
        cnt = jnp.clip(n_valid, 1, w).astype(F32)
        d = win / cnt - u
        y = jnp.dot(d.astype(BF16), w_ref[gi], preferred_element_type=F32) * sc_ref[:, cols]
        o_ref[:, cols] = y.astype(o_ref.dtype)


def _pool(z, pool_w, pool_scale):
    b, l, _ = z.shape
    tl = _pick(l, (640, 128))
    hb = tl // MAX_WINDOW
    return pl.pallas_call(
        _pool_kernel,
        grid=(b, l // tl),
        in_specs=[pl.BlockSpec((None, tl, POOL_WIDTH), lambda bi, t: (bi, t, 0)),
                  pl.BlockSpec((None, MAX_WINDOW, POOL_WIDTH),
                               lambda bi, t: (bi, jnp.maximum(t * hb - 1, 0), 0)),
                  pl.BlockSpec((len(POOL_WINDOWS), POOL_GROUP, POOL_GROUP), lambda bi, t: (0, 0, 0)),
                  pl.BlockSpec((1, POOL_WIDTH), lambda bi, t: (0, 0))],
        out_specs=pl.BlockSpec((None, tl, POOL_WIDTH), lambda bi, t: (bi, t, 0)),
        out_shape=jax.ShapeDtypeStruct((b, l, POOL_WIDTH), BF16),
        scratch_shapes=[pltpu.VMEM((tl + MAX_WINDOW, POOL_WIDTH), F32)],
        compiler_params=_cparams(("parallel", "parallel")),
        name="pool_mixer",
    )(z, z, pool_w, pool_scale)


def _attn_kernel(lam_ref, q_ref, k_ref, v_ref, sub_ref, o_ref, m_scr, l_scr, acc_scr, nb_scr,
                 *, tq, lambda_init):
    h = pl.program_id(1)
    i = pl.program_id(2)
    slope = jnp.exp2(-2.0 * (h + 1).astype(F32))
    lane = lax.broadcasted_iota(jnp.int32, (tq, 2 * DIFF_QK_DIM), 1)
    q = q_ref[...] * jnp.asarray(DIFF_QK_DIM ** -0.5, BF16)
    zero = jnp.zeros_like(q)
    q_maps = (jnp.where(lane < DIFF_QK_DIM, q, zero), jnp.where(lane >= DIFF_QK_DIM, q, zero))
    rc = (lax.broadcasted_iota(jnp.int32, (tq, tq), 0)
          - lax.broadcasted_iota(jnp.int32, (tq, tq), 1))
    nb_scr[...] = -slope * rc.astype(F32)
    m_scr[...] = jnp.full(m_scr.shape, NEG_BIG, F32)
    l_scr[...] = jnp.zeros(l_scr.shape, F32)
    acc_scr[...] = jnp.zeros(acc_scr.shape, F32)

    def tile(j, causal, keymask):
        start = pl.multiple_of(j * tq, tq)
        kt = k_ref[pl.ds(start, tq), :]
        vt = v_ref[pl.ds(start, tq), :]
        bias = nb_scr[...] - slope * ((i - j) * tq).astype(F32)
        allowed = None
        if causal:
            allowed = rc >= 0
        if keymask:
            kcol = lax.broadcasted_iota(jnp.int32, (tq, tq), 1)
            ok = kcol >= FIRST_VALID
            if causal:
                ok = jnp.logical_or(ok, rc == 0)
                allowed = jnp.logical_and(allowed, ok)
            else:
                allowed = ok
        for mp in range(2):
            s = lax.dot_general(q_maps[mp], kt, (((1,), (1,)), ((), ())),
                                preferred_element_type=F32) + bias
            if allowed is not None:
                s = jnp.where(allowed, s, NEG_BIG)
            m_prev = m_scr[mp]
            m_new = jnp.maximum(m_prev, jnp.max(s, axis=-1, keepdims=True))
            alpha = jnp.exp(m_prev - m_new)
            p = jnp.exp(s - m_new)
            l_scr[mp] = alpha * l_scr[mp] + jnp.sum(p, axis=-1, keepdims=True)
            acc_scr[mp] = alpha * acc_scr[mp] + jnp.dot(p.astype(BF16), vt, preferred_element_type=F32)
            m_scr[mp] = m_new

    @pl.when(i == 0)
    def _():
        tile(0, True, True)

    @pl.when(i > 0)
    def _():
        tile(0, False, True)
        lax.fori_loop(1, i, lambda j, c: (tile(j, False, False), c)[1], 0)
        tile(i, True, False)

    lam = (jnp.exp(jnp.sum(lam_ref[0:1, :] * lam_ref[1:2, :], axis=-1, keepdims=True))
           - jnp.exp(jnp.sum(lam_ref[2:3, :] * lam_ref[3:4, :], axis=-1, keepdims=True)) + lambda_init)
    o = acc_scr[0] / l_scr[0] - lam * (acc_scr[1] / l_scr[1])
    o = o * lax.rsqrt(jnp.mean(o * o, axis=-1, keepdims=True) + 1e-6) * sub_ref[...] * (1.0 - lambda_init)
    o_ref[...] = o.astype(o_ref.dtype)


def _attn(z, lam_params, subln, lambda_init):
    b, l, _ = z.shape
    tq = _pick(l, (640, 128))
    qb, kb, vb = POOL_WIDTH // LANES, (POOL_WIDTH + DIFF_WIDTH) // LANES, (POOL_WIDTH + 2 * DIFF_WIDTH) // LANES
    return pl.pallas_call(
        functools.partial(_attn_kernel, tq=tq, lambda_init=lambda_init),
        grid=(b, DIFF_HEADS, l // tq),
        in_specs=[pl.BlockSpec((4, DIFF_QK_DIM), lambda bi, h, i: (0, 0)),
                  pl.BlockSpec((None, tq, LANES), lambda bi, h, i: (bi, i, qb + h)),
                  pl.BlockSpec((None, l, LANES), lambda bi, h, i: (bi, 0, kb + h)),
                  pl.BlockSpec((None, l, LANES), lambda bi, h, i: (bi, 0, vb + h)),
                  pl.BlockSpec((1, DIFF_V_DIM), lambda bi, h, i: (0, 0))],
        out_specs=pl.BlockSpec((None, tq, DIFF_V_DIM), lambda bi, h, i: (bi, i, h)),
        out_shape=jax.ShapeDtypeStruct((b, l, DIFF_WIDTH), BF16),
        scratch_shapes=[pltpu.VMEM((2, tq, 1), F32), pltpu.VMEM((2, tq, 1), F32),
                        pltpu.VMEM((2, tq, DIFF_V_DIM), F32), pltpu.VMEM((tq, tq), F32)],
        compiler_params=_cparams(("parallel", "parallel", "arbitrary")),
        name="diff_attn",
    )(lam_params, z, z, z, subln)


def _gla_kernel(q_ref, k_ref, v_ref, r_ref, glr_ref, w2_ref, gb_ref, hn_ref, o_ref, st_ref, *, c_len, sb):
    c = pl.program_id(1)

    @pl.when(c == 0)
    def _():
        st_ref[...] = jnp.zeros(st_ref.shape, F32)

    pos = c * c_len + lax.broadcasted_iota(jnp.int32, (c_len, 1), 0)
    validf = (pos >= FIRST_VALID).astype(F32)
    ri = lax.broadcasted_iota(jnp.int32, (c_len, c_len), 0)
    ci = lax.broadcasted_iota(jnp.int32, (c_len, c_len), 1)
    tri = jnp.where(ri >= ci, 1.0, 0.0).astype(BF16)
    row = lax.broadcasted_iota(jnp.int32, (c_len, 1), 0)
    t_loc = lax.broadcasted_iota(jnp.int32, (sb, 1), 0)
    lane_c = lax.broadcasted_iota(jnp.int32, (sb, c_len), 1)
    glr = glr_ref[...]
    nt = (((1,), (1,)), ((), ()))
    tn = (((0,), (0,)), ((), ()))

    for h in range(GLA_HEADS):
        ks = slice(h * GLA_DK, (h + 1) * GLA_DK)
        vs = slice(h * GLA_DV, (h + 1) * GLA_DV)
        g = jnp.dot(glr, w2_ref[:, ks], preferred_element_type=F32) + gb_ref[:, ks]
        log_a = -(jnp.maximum(-g, 0.0) + jnp.log(1.0 + jnp.exp(-jnp.abs(g)))) * (1.0 / GLA_TAU)
        la_hi = log_a.astype(BF16)
        la_lo = (log_a - la_hi.astype(F32)).astype(BF16)
        b = (jnp.dot(tri, la_hi, preferred_element_type=F32)
             + jnp.dot(tri, la_lo, preferred_element_type=F32))
        q = q_ref[:, ks].astype(F32) * (GLA_DK ** -0.5)
        k = k_ref[:, ks].astype(F32) * validf
        v = v_ref[:, vs]
        st = st_ref[h]

        blocks = []
        for i in range(c_len // sb):
            lo = i * sb
            q_i = q[lo:lo + sb]
            b_i = b[lo:lo + sb]
            a_i = jnp.zeros((sb, c_len), F32)
            if i > 0:
                b_ref_row = b[lo - 1:lo]
                q_t = q_i * jnp.exp(b_i - b_ref_row)
                k_t = jnp.where(row < lo, k * jnp.exp(jnp.minimum(b_ref_row - b, 0.0)), 0.0)
                a_i = lax.dot_general(q_t.astype(BF16), k_t.astype(BF16), nt, preferred_element_type=F32)
            for s in range(sb):
                r = lo + s
                e = jnp.exp(jnp.minimum(b_i - b[r:r + 1], 0.0))
                col = jnp.sum(q_i * k[r:r + 1] * e, axis=-1, keepdims=True)
                col = jnp.where(t_loc >= s, col, 0.0)
                a_i = jnp.where(lane_c == r, col, a_i)
            blocks.append(a_i)
        att = jnp.concatenate(blocks, axis=0)

        o = jnp.dot(att.astype(BF16), v, preferred_element_type=F32)
        o = o + lax.dot_general((q * jnp.exp(b)).astype(BF16), st.astype(BF16), nt,
                                preferred_element_type=F32)
        b_last = b[c_len - 1:c_len]
        k_hat = (k * jnp.exp(b_last - b)).astype(BF16)
        st_ref[h] = st * jnp.exp(b_last) + lax.dot_general(v, k_hat, tn, preferred_element_type=F32)

        o = o * lax.rsqrt(jnp.mean(o * o, axis=-1, keepdims=True) + 1e-6) * hn_ref[...]
        rg = r_ref[:, vs].astype(F32)
        o_ref[:, vs] = (o * (rg * jax.nn.sigmoid(rg))).astype(o_ref.dtype)


def _gla(z, w2, gate_b, head_norm):
    b, l, _ = z.shape
    c_len = 64
    qw, vw = GLA_QK_WIDTH, GLA_V_WIDTH
    return pl.pallas_call(
        functools.partial(_gla_kernel, c_len=c_len, sb=16),
        grid=(b, l // c_len),
        in_specs=[pl.BlockSpec((None, c_len, qw), lambda bi, c: (bi, c, 0)),
                  pl.BlockSpec((None, c_len, qw), lambda bi, c: (bi, c, 1)),
                  pl.BlockSpec((None, c_len, vw), lambda bi, c: (bi, c, 1)),
                  pl.BlockSpec((None, c_len, vw), lambda bi, c: (bi, c, 2)),
                  pl.BlockSpec((None, c_len, LANES), lambda bi, c: (bi, c, (2 * qw + 2 * vw) // LANES)),
                  pl.BlockSpec((LANES, qw), lambda bi, c: (0, 0)),
                  pl.BlockSpec((1, qw), lambda bi, c: (0, 0)),
                  pl.BlockSpec((1, GLA_DV), lambda bi, c: (0, 0))],
        out_specs=pl.BlockSpec((None, c_len, vw), lambda bi, c: (bi, c, 0)),
        out_shape=jax.ShapeDtypeStruct((b, l, vw), BF16),
        scratch_shapes=[pltpu.VMEM((GLA_HEADS, GLA_DV, GLA_DK), F32)],
        compiler_params=_cparams(("parallel", "arbitrary")),
        name="gla_mixer",
    )(z, z, z, z, z, w2, gate_b, head_norm)


def _mixout_kernel(*refs):
    *y_refs, w_ref, h_ref, g_ref, b_ref, o_ref = refs
    acc = ALPHA * h_ref[...]
    off = 0
    for y_ref in y_refs:
        kd = y_ref.shape[1]
        acc = acc + jnp.dot(y_ref[...], w_ref[off:off + kd, :], preferred_element_type=F32)
        off += kd
    o_ref[...] = _layer_norm(acc, g_ref[...], b_ref[...])


def _mixout(ys, w, h2d, g, b, name):
    n, d = h2d.shape
    tm = _pick(n, (640, 512, 256, 128))
    return pl.pallas_call(
        _mixout_kernel,
        grid=(n // tm,),
        in_specs=[pl.BlockSpec((tm, y.shape[1]), lambda i: (i, 0)) for y in ys]
        + [pl.BlockSpec(w.shape, lambda i: (0, 0)),
           pl.BlockSpec((tm, d), lambda i: (i, 0)),
           pl.BlockSpec((1, d), lambda i: (0, 0)),
           pl.BlockSpec((1, d), lambda i: (0, 0))],
        out_specs=pl.BlockSpec((tm, d), lambda i: (i, 0)),
        out_shape=jax.ShapeDtypeStruct((n, d), F32),
        compiler_params=_cparams(("parallel",)),
        name=name,
    )(*ys, w, h2d, g, b)


def _router_kernel(bias_ref, h_ref, rw_ref, haug_ref, pos_ref, cnt_ref, upper_scr, cnt_scr, *, cap):
    tm, d = h_ref.shape

    @pl.when(pl.program_id(0) == 0)
    def _():
        r = lax.broadcasted_iota(jnp.int32, (tm, tm), 0)
        c = lax.broadcasted_iota(jnp.int32, (tm, tm), 1)
        upper_scr[...] = jnp.where(r < c, 1.0, 0.0).astype(BF16)
        cnt_scr[...] = jnp.zeros(cnt_scr.shape, F32)

    x = h_ref[...]
    w = rw_ref[...]
    xh = x.astype(BF16)
    xl = (x - xh.astype(F32)).astype(BF16)
    wh = w.astype(BF16)
    wl = (w - wh.astype(F32)).astype(BF16)
    nt = (((1,), (1,)), ((), ()))
    logits = (lax.dot_general(wh, xh, nt, preferred_element_type=F32)
              + lax.dot_general(wh, xl, nt, preferred_element_type=F32)
              + lax.dot_general(wl, xh, nt, preferred_element_type=F32))
    lg = [logits[e:e + 1, :] for e in range(N_EXPERTS)]
    mx = functools.reduce(jnp.maximum, lg)
    ex = [jnp.exp(v - mx) for v in lg]
    den = functools.reduce(jnp.add, ex)
    probs = [v / den for v in ex]
    sel = [probs[e] + bias_ref[e] for e in range(N_EXPERTS)]

    def top2_sum(a, b, c, d_):
        hi1, lo1 = jnp.maximum(a, b), jnp.minimum(a, b)
        hi2, lo2 = jnp.maximum(c, d_), jnp.minimum(c, d_)
        return jnp.maximum(hi1, hi2) + jnp.maximum(jnp.minimum(hi1, hi2), jnp.maximum(lo1, lo2))

    best = top2_sum(*sel[0:4])
    gidx = jnp.zeros((1, tm), jnp.int32)
    for g in range(1, N_GROUPS):
        sc = top2_sum(*sel[4 * g:4 * g + 4])
        better = sc > best
        gidx = jnp.where(better, g, gidx)
        best = jnp.maximum(best, sc)

    def pick(vals, j):
        out = vals[j]
        for g in range(1, N_GROUPS):
            out = jnp.where(gidx == g, vals[4 * g + j], out)
        return out

    sg = [pick(sel, j) for j in range(EXPERTS_PER_GROUP)]
    pg = [pick(probs, j) for j in range(EXPERTS_PER_GROUP)]
    chosen = []
    for j in range(EXPERTS_PER_GROUP):
        rank = jnp.zeros((1, tm), jnp.int32)
        for o in range(EXPERTS_PER_GROUP):
            if o == j:
                continue
            ahead = (sg[o] > sg[j]) if o > j else (sg[o] >= sg[j])
            rank = rank + ahead.astype(jnp.int32)
        chosen.append(jnp.where(rank < 2, pg[j], 0.0))
    wsum = functools.reduce(jnp.add, chosen)
    wn = [cj / wsum for cj in chosen]

    onehot = [jnp.where(gidx == g, 1.0, 0.0) for g in range(N_GROUPS)]
    oh8 = jnp.concatenate(onehot + [jnp.zeros((8 - N_GROUPS, tm), F32)], axis=0)
    before = jnp.dot(oh8.astype(BF16), upper_scr[...], preferred_element_type=F32)
    seen = before + cnt_scr[:, 0:1]
    rank_in_group = functools.reduce(jnp.add, [onehot[g] * seen[g:g + 1, :] for g in range(N_GROUPS)])
    pos_ref[...] = gidx * cap + rank_in_group.astype(jnp.int32)
    cnt_scr[...] = cnt_scr[...] + jnp.sum(oh8, axis=1, keepdims=True)
    cnt_ref[...] = cnt_scr[...]

    pay_t = jnp.concatenate(wn + [jnp.zeros((LANES - EXPERTS_PER_GROUP, tm), F32)], axis=0)
    haug_ref[:, 0:d] = x
    haug_ref[:, d:] = pay_t.T


def _router(h2d, router_w_t, router_bias, cap):
    n, d = h2d.shape
    tm = _pick(n, (640, 512, 256, 128))
    return pl.pallas_call(
        functools.partial(_router_kernel, cap=cap),
        grid=(n // tm,),
        in_specs=[pl.BlockSpec(memory_space=pltpu.SMEM),
                  pl.BlockSpec((tm, d), lambda i: (i, 0)),
                  pl.BlockSpec((N_EXPERTS, d), lambda i: (0, 0))],
        out_specs=[pl.BlockSpec((tm, d + LANES), lambda i: (i, 0)),
                   pl.BlockSpec((None, 1, tm), lambda i: (i, 0, 0)),
                   pl.BlockSpec((8, LANES), lambda i: (0, 0))],
        out_shape=[jax.ShapeDtypeStruct((n, d + LANES), F32),
                   jax.ShapeDtypeStruct((n // tm, 1, tm), jnp.int32),
                   jax.ShapeDtypeStruct((8, LANES), F32)],
        scratch_shapes=[pltpu.VMEM((tm, tm), BF16), pltpu.VMEM((8, LANES), F32)],
        compiler_params=_cparams(("arbitrary",)),
        name="router",
    )(router_bias, h2d, router_w_t)


def _row_dma(src_ref, dst_ref, sem, src_row, dst_row):
    return pltpu.make_async_copy(src_ref.at[pl.ds(src_row, 1), :], dst_ref.at[pl.ds(dst_row, 1), :], sem)


def _scatter_rows_kernel(idx_ref, src_ref, dst_ref, sem, *, rows):
    base = pl.program_id(0) * rows

    def issue(r, c):
        _row_dma(src_ref, dst_ref, sem, base + r, idx_ref[base + r]).start()
        return c

    def drain(r, c):
        _row_dma(src_ref, dst_ref, sem, 0, 0).wait()
        return c

    lax.fori_loop(0, rows, issue, 0, unroll=8)
    lax.fori_loop(0, rows, drain, 0, unroll=8)


def _gather_rows_kernel(idx_ref, src_ref, dst_ref, sem, *, rows):
    base = pl.program_id(0) * rows

    def issue(r, c):
        _row_dma(src_ref, dst_ref, sem, idx_ref[base + r], base + r).start()
        return c

    def drain(r, c):
        _row_dma(src_ref, dst_ref, sem, 0, 0).wait()
        return c

    lax.fori_loop(0, rows, issue, 0, unroll=8)
    lax.fori_loop(0, rows, drain, 0, unroll=8)


def _permute_rows(body, idx, src, n_dst, name):
    m = idx.shape[0]
    rows = _pick(m, (512, 256, 128))
    return pl.pallas_call(
        functools.partial(body, rows=rows),
        grid_spec=pltpu.PrefetchScalarGridSpec(
            num_scalar_prefetch=1,
            grid=(m // rows,),
            in_specs=[pl.BlockSpec(memory_space=pl.ANY)],
            out_specs=pl.BlockSpec(memory_space=pl.ANY),
            scratch_shapes=[pltpu.SemaphoreType.DMA(())]),
        out_shape=jax.ShapeDtypeStruct((n_dst, src.shape[1]), src.dtype),
        compiler_params=_cparams(("arbitrary",)),
        name=name,
    )(idx, src)


def _moe_kernel(grp_ref, blk_ref, nv_ref, x_ref, wg_ref, wu_ref, wd_ref, g_ref, b_ref, o_ref):
    nv = nv_ref[pl.program_id(0)]

    @pl.when(nv > 0)
    def _():
        tm, d = o_ref.shape
        live = lax.broadcasted_iota(jnp.int32, (tm, 1), 0) < nv
        x = jnp.where(live, x_ref[:, 0:d], 0.0)
        cw = jnp.where(live, x_ref[:, d:], 0.0)
        xb = x.astype(BF16)
        acc = ALPHA * x
        for j in range(EXPERTS_PER_GROUP):
            gate = jnp.dot(xb, wg_ref[j], preferred_element_type=F32)
            up = jnp.dot(xb, wu_ref[j], preferred_element_type=F32)
            act = gate * jax.nn.sigmoid(gate) * up * cw[:, j:j + 1]
            acc = acc + jnp.dot(act.astype(BF16), wd_ref[j], preferred_element_type=F32)
        o_ref[...] = _layer_norm(acc, g_ref[...], b_ref[...])


def _moe(x_sorted, tables, wg, wu, wd, g, b, tm, name):
    rows, wa = x_sorted.shape
    d = wa - LANES
    n_tiles = tables[0].shape[0]
    epg = EXPERTS_PER_GROUP
    return pl.pallas_call(
        _moe_kernel,
        grid_spec=pltpu.PrefetchScalarGridSpec(
            num_scalar_prefetch=3,
            grid=(n_tiles,),
            in_specs=[pl.BlockSpec((tm, wa), lambda t, grp, blk, nv: (blk[t], 0)),
                      pl.BlockSpec((epg, d, D_EXPERT), lambda t, grp, blk, nv: (grp[t], 0, 0)),
                      pl.BlockSpec((epg, d, D_EXPERT), lambda t, grp, blk, nv: (grp[t], 0, 0)),
                      pl.BlockSpec((epg, D_EXPERT, d), lambda t, grp, blk, nv: (grp[t], 0, 0)),
                      pl.BlockSpec((1, d), lambda t, grp, blk, nv: (0, 0)),
                      pl.BlockSpec((1, d), lambda t, grp, blk, nv: (0, 0))],
            out_specs=pl.BlockSpec((tm, d), lambda t, grp, blk, nv: (blk[t], 0))),
        out_shape=jax.ShapeDtypeStruct((rows, d), F32),
        compiler_params=_cparams(("arbitrary",)),
        name=name,
    )(*tables, x_sorted, wg, wu, wd, g, b)


def _tile_tables(counts, tm, cap, n_tiles):
    tiles_g = (counts + tm - 1) // tm
    ends = jnp.cumsum(tiles_g)
    starts = ends - tiles_g
    t = jnp.arange(n_tiles, dtype=jnp.int32)
    tc = jnp.minimum(t, ends[-1] - 1)
    grp = jnp.sum((tc[:, None] >= ends[None, :]).astype(jnp.int32), axis=1)
    local = tc - starts[grp]
    blk = grp * (cap // tm) + local
    nv = jnp.where(t < ends[-1], jnp.clip(counts[grp] - local * tm, 0, tm), 0)
    return grp.astype(jnp.int32), blk.astype(jnp.int32), nv.astype(jnp.int32)


def _routed_moe(h2d, router_w_t, router_bias, wg, wu, wd, g, b, out_idx_fn, n_out, name):
    n, d = h2d.shape
    cap = n
    tm = _pick(n, (512, 256, 128))
    haug, pos, cnt = _router(h2d, router_w_t, router_bias, cap)
    pos = pos.reshape(n)
    counts = cnt[:N_GROUPS, 0].astype(jnp.int32)
    x_sorted = _permute_rows(_scatter_rows_kernel, pos, haug, N_GROUPS * cap, name + "_scatter")
    tables = _tile_tables(counts, tm, cap, n // tm + N_GROUPS)
    y_sorted = _moe(x_sorted, tables, wg, wu, wd, g, b, tm, name)
    return _permute_rows(_gather_rows_kernel, out_idx_fn(pos), y_sorted, n_out, name + "_gather")


def kernel(x, meta, even_w_in, pool_w, pool_scale, diff_lq1, diff_lk1, diff_lq2, diff_lk2, diff_subln,
           even_w_out, odd_w_in, gla_gate_w2, gla_gate_b, gla_head_norm, odd_w_out, ln_mix_g, ln_mix_b,
           ln_ffn_g, ln_ffn_b, router_w, router_bias, moe_w_gate, moe_w_up, moe_w_down):
    bsz, seq, d = x.shape
    l = PREFIX + seq
    n = bsz * l
    pad = jnp.zeros((bsz, FIRST_VALID, d), x.dtype)
    metas = jnp.broadcast_to(meta.astype(x.dtype)[None], (bsz, N_META, d))
    h = jnp.concatenate([pad, metas, x], axis=1).reshape(n, d)
    router_w_t = router_w.T
    row = lambda a: a.reshape(1, -1)

    for i in range(DEPTH):
        j = i // 2
        if i % 2 == 0:
            lambda_init = 0.8 - 0.6 * math.exp(-0.3 * i)
            z = _proj(h, even_w_in[j].astype(BF16), "even_in_proj").reshape(bsz, l, EVEN_IN)
            y_pool = _pool(z, pool_w[j].astype(BF16), row(pool_scale[j]))
            lam_params = jnp.stack([diff_lq1[j], diff_lk1[j], diff_lq2[j], diff_lk2[j]])
            y_attn = _attn(z, lam_params, row(diff_subln[j]), lambda_init)
            ys = [y_pool.reshape(n, POOL_WIDTH), y_attn.reshape(n, DIFF_WIDTH)]
            h = _mixout(ys, even_w_out[j].astype(BF16), h, row(ln_mix_g[i]), row(ln_mix_b[i]), "even_out_proj")
        else:
            w_in = jnp.pad(odd_w_in[j], ((0, 0), (0, ODD_IN_PAD - ODD_IN))).astype(BF16)
            z = _proj(h, w_in, "odd_in_proj").reshape(bsz, l, ODD_IN_PAD)
            w2 = jnp.pad(gla_gate_w2[j], ((0, LANES - GLA_RANK), (0, 0))).astype(BF16)
            y = _gla(z, w2, row(gla_gate_b[j]), row(gla_head_norm[j]))
            h = _mixout([y.reshape(n, GLA_V_WIDTH)], odd_w_out[j].astype(BF16), h,
                        row(ln_mix_g[i]), row(ln_mix_b[i]), "odd_out_proj")
        last = i == DEPTH - 1
        if last:
            out_idx_fn = lambda pos: pos.reshape(bsz, l)[:, PREFIX:].reshape(-1)
        else:
            out_idx_fn = lambda pos: pos
        h = _routed_moe(h, router_w_t, router_bias, moe_w_gate[i].astype(BF16), moe_w_up[i].astype(BF16),
                        moe_w_down[i].astype(BF16), row(ln_ffn_g[i]), row(ln_ffn_b[i]),
                        out_idx_fn, bsz * seq if last else n, f"moe_{i}")
    return h.reshape(bsz, seq, d)
```

```python
import functools
import math

import jax
import jax.numpy as jnp
from jax import lax
from jax.experimental import pallas as pl
from jax.experimental.pallas import tpu as pltpu

F32 = jnp.float32
BF16 = jnp.bfloat16

D_MODEL = 1024
DEPTH = 2
N_META = 16
PREFIX = 128
FIRST_VALID = PREFIX - N_META
POOL_WINDOWS = (2, 4, 8, 16)
POOL_GROUP = 128
POOL_WIDTH = 512
MAX_WINDOW = 16
DIFF_HEADS = 4
DIFF_QK_DIM = 64
DIFF_V_DIM = 128
DIFF_WIDTH = 512
EVEN_IN = 2048
GLA_HEADS = 4
GLA_DK = 128
GLA_DV = 256
GLA_RANK = 16
GLA_TAU = 16.0
GLA_QK_WIDTH = GLA_HEADS * GLA_DK
GLA_V_WIDTH = GLA_HEADS * GLA_DV
ODD_IN = 2 * GLA_QK_WIDTH + 2 * GLA_V_WIDTH + GLA_RANK
ODD_IN_PAD = 3200
N_EXPERTS = 16
N_GROUPS = 4
EXPERTS_PER_GROUP = 4
D_EXPERT = 512
ALPHA = (2.0 * DEPTH) ** 0.25
NEG_BIG = -1e30
KT = 256

LANES = 128
VMEM_LIMIT = 56 * 1024 * 1024


def _pick(n, candidates):
    for c in candidates:
        if n % c == 0:
            return c
    raise ValueError(f"no tile for {n} in {candidates}")


def _cparams(sem):
    return pltpu.CompilerParams(dimension_semantics=sem, vmem_limit_bytes=VMEM_LIMIT)


def _layer_norm(a, g, b):
    mu = jnp.mean(a, axis=-1, keepdims=True)
    d = a - mu
    var = jnp.mean(d * d, axis=-1, keepdims=True)
    return d * lax.rsqrt(var + 1e-5) * g + b


def _proj_kernel(x_ref, w_ref, o_ref, *, tn):
    xb = x_ref[...].astype(BF16)
    for j in range(o_ref.shape[1] // tn):
        cols = slice(j * tn, (j + 1) * tn)
        o_ref[:, cols] = jnp.dot(xb, w_ref[:, cols], preferred_element_type=F32).astype(o_ref.dtype)


def _proj(x2d, w, name):
    n, k = x2d.shape
    wout = w.shape[1]
    tm = _pick(n, (640, 512, 256, 128))
    tn = _pick(wout, (640, 512, 128))
    return pl.pallas_call(
        functools.partial(_proj_kernel, tn=tn),
        grid=(n // tm,),
        in_specs=[pl.BlockSpec((tm, k), lambda i: (i, 0)),
                  pl.BlockSpec((k, wout), lambda i: (0, 0))],
        out_specs=pl.BlockSpec((tm, wout), lambda i: (i, 0)),
        out_shape=jax.ShapeDtypeStruct((n, wout), BF16),
        compiler_params=_cparams(("parallel",)),
        name=name,
    )(x2d, w)


def _pool_kernel(cur_ref, halo_ref, w_ref, sc_ref, o_ref, u_scr):
    t = pl.program_id(1)
    tl = cur_ref.shape[0]
    pos = t * tl + lax.broadcasted_iota(jnp.int32, (tl, 1), 0)
    hpos = t * tl - MAX_WINDOW + lax.broadcasted_iota(jnp.int32, (MAX_WINDOW, 1), 0)
    u_scr[0:MAX_WINDOW, :] = jnp.where(hpos >= FIRST_VALID, halo_ref[...].astype(F32), 0.0)
    u_scr[MAX_WINDOW:, :] = jnp.where(pos >= FIRST_VALID, cur_ref[...].astype(F32), 0.0)
    n_valid = pos - (FIRST_VALID - 1)
    for gi, w in enumerate(POOL_WINDOWS):
        cols = slice(gi * POOL_GROUP, (gi + 1) * POOL_GROUP)
        u = u_scr[MAX_WINDOW:MAX_WINDOW + tl, cols]
        win = u
        for j in range(1, w):
            win = win + u_scr[MAX_WINDOW - j:MAX_WINDOW - j + tl, cols]
        cnt = jnp.clip(n_valid, 1, w).astype(F32)
        d = win / cnt - u
        y = jnp.dot(d.astype(BF16), w_ref[gi], preferred_element_type=F32) * sc_ref[:, cols]
        o_ref[:, cols] = y.astype(o_ref.dtype)


def _pool(z, pool_w, pool_scale):
    b, l, _ = z.shape
    tl = _pick(l, (640, 128))
    hb = tl // MAX_WINDOW
    return pl.pallas_call(
        _pool_kernel,
        grid=(b, l // tl),
        in_specs=[pl.BlockSpec((None, tl, POOL_WIDTH), lambda bi, t: (bi, t, 0)),
                  pl.BlockSpec((None, MAX_WINDOW, POOL_WIDTH),
                               lambda bi, t: (bi, jnp.maximum(t * hb - 1, 0), 0)),
                  pl.BlockSpec((len(POOL_WINDOWS), POOL_GROUP, POOL_GROUP), lambda bi, t: (0, 0, 0)),
                  pl.BlockSpec((1, POOL_WIDTH), lambda bi, t: (0, 0))],
        out_specs=pl.BlockSpec((None, tl, POOL_WIDTH), lambda bi, t: (bi, t, 0)),
        out_shape=jax.ShapeDtypeStruct((b, l, POOL_WIDTH), BF16),
        scratch_shapes=[pltpu.VMEM((tl + MAX_WINDOW, POOL_WIDTH), F32)],
        compiler_params=_cparams(("parallel", "parallel")),
        name="pool_mixer",
    )(z, z, pool_w, pool_scale)


def _attn_kernel(lam_ref, q_ref, k_ref, v_ref, sub_ref, o_ref, ka_scr, kb_scr, vt_scr,
                 s0_scr, s1_scr, t0_scr, t1_scr, m0_scr, m1_scr, l0_scr, l1_scr, acc0_scr, acc1_scr,
                 *, tq, lambda_init):
    h = pl.program_id(1)
    i = pl.program_id(2)
    slope = jnp.exp2(-2.0 * (h + 1).astype(F32))
    half = DIFF_QK_DIM
    nt = (((1,), (1,)), ((), ()))
    n_chunks = k_ref.shape[0] // LANES

    @pl.when(i == 0)
    def _():
        lane = lax.broadcasted_iota(jnp.int32, (LANES, 2 * half), 1)
        rowf = lax.broadcasted_iota(jnp.int32, (LANES, 2 * half), 0).astype(F32)

        def build(c, carry):
            rows = pl.ds(pl.multiple_of(c * LANES, LANES), LANES)
            kt = k_ref[rows, :].astype(F32)
            hi = jnp.full((LANES, 2 * half), c, jnp.int32).astype(F32)
            ka = jnp.where(lane < half, kt, jnp.where(lane == half, hi, jnp.where(lane == half + 1, rowf, 0.0)))
            kb = jnp.where(lane >= half, kt, jnp.where(lane == 0, hi, jnp.where(lane == 1, rowf, 0.0)))
            ka_scr[rows, :] = ka.astype(BF16)
            kb_scr[rows, :] = kb.astype(BF16)
            vt_scr[c] = v_ref[rows, :].astype(F32).T.astype(BF16)
            return carry

        lax.fori_loop(0, n_chunks, build, 0)
        pad_rows = pl.ds(n_chunks * LANES, LANES)
        ka_scr[pad_rows, :] = jnp.zeros((LANES, 2 * half), BF16)
        kb_scr[pad_rows, :] = jnp.zeros((LANES, 2 * half), BF16)
        vt_scr[n_chunks] = jnp.zeros((DIFF_V_DIM, LANES), BF16)

    lane = lax.broadcasted_iota(jnp.int32, (tq, 2 * half), 1)
    q = (q_ref[...] * jnp.asarray(half ** -0.5, BF16)).astype(F32)
    f_hi = LANES * slope
    qa = jnp.where(lane < half, q, jnp.where(lane == half, f_hi, jnp.where(lane == half + 1, slope, 0.0)))
    qb = jnp.where(lane >= half, q, jnp.where(lane == 0, f_hi, jnp.where(lane == 1, slope, 0.0)))
    q_maps = (qa.astype(BF16), qb.astype(BF16))
    k_maps = (ka_scr, kb_scr)
    s_scrs, t_scrs = (s0_scr, s1_scr), (t0_scr, t1_scr)
    m_scrs, l_scrs, acc_scrs = (m0_scr, m1_scr), (l0_scr, l1_scr), (acc0_scr, acc1_scr)
    for mp in range(2):
        m_scrs[mp][...] = jnp.full(m0_scr.shape, NEG_BIG, F32)
        l_scrs[mp][...] = jnp.zeros(l0_scr.shape, F32)
        acc_scrs[mp][...] = jnp.zeros(acc0_scr.shape, F32)
    q0 = i * tq
    n_full = q0 // KT

    def scores(t, mode):
        k0 = pl.multiple_of(t * KT, KT)
        buf = t % 2
        allowed = None
        if mode is not None:
            kpos = k0 + lax.broadcasted_iota(jnp.int32, (KT, tq), 0)
            allowed = kpos >= FIRST_VALID
            if mode == "diag":
                qpos = q0 + lax.broadcasted_iota(jnp.int32, (KT, tq), 1)
                allowed = jnp.logical_and(kpos <= qpos, jnp.logical_or(allowed, kpos == qpos))
        for mp in range(2):
            s = lax.dot_general(k_maps[mp][pl.ds(k0, KT), :], q_maps[mp], nt, preferred_element_type=F32)
            if allowed is not None:
                s = jnp.where(allowed, s, NEG_BIG)
            s_scrs[mp][buf] = s
            t_scrs[mp][buf] = jnp.max(s, axis=0, keepdims=True)

    def accumulate(t):
        c0 = t * (KT // LANES)
        buf = t % 2
        vt = jnp.concatenate([vt_scr[c0 + u] for u in range(KT // LANES)], axis=1)
        for mp in range(2):
            m_prev = m_scrs[mp][...]
            m_new = jnp.maximum(m_prev, t_scrs[mp][buf])
            alpha = jnp.exp(m_prev - m_new)
            p = jnp.exp(s_scrs[mp][buf] - m_new)
            l_scrs[mp][...] = alpha * l_scrs[mp][...] + jnp.sum(p, axis=0, keepdims=True)
            acc_scrs[mp][...] = alpha * acc_scrs[mp][...] + jnp.dot(vt, p.astype(BF16), preferred_element_type=F32)
            m_scrs[mp][...] = m_new

    @pl.when(n_full >= 1)
    def _():
        scores(0, "valid")

    @pl.when(n_full == 0)
    def _():
        scores(0, "diag")

    def steady(t, c):
        accumulate(t)
        scores(t + 1, None)
        return c

    lax.fori_loop(0, n_full - 1, steady, 0)

    @pl.when(n_full >= 1)
    def _():
        accumulate(n_full - 1)
        scores(n_full, "diag")

    accumulate(n_full)
    scores(n_full + 1, "diag")
    accumulate(n_full + 1)
    scores(n_full + 2, "diag")
    accumulate(n_full + 2)

    lam = (jnp.exp(jnp.sum(lam_ref[0:1, :] * lam_ref[1:2, :], axis=-1, keepdims=True))
           - jnp.exp(jnp.sum(lam_ref[2:3, :] * lam_ref[3:4, :], axis=-1, keepdims=True)) + lambda_init)
    o = acc0_scr[...] / l0_scr[...] - lam * (acc1_scr[...] / l1_scr[...])
    o = o * lax.rsqrt(jnp.mean(o * o, axis=0, keepdims=True) + 1e-6) * sub_ref[...] * (1.0 - lambda_init)
    o_ref[...] = o.T.astype(o_ref.dtype)


def _attn(z, lam_params, subln_col, lambda_init):
    b, l, _ = z.shape
    tq = 5 * LANES
    assert l % tq == 0 and l % LANES == 0 and 2 * KT < tq <= 3 * KT - LANES
    qb, kb, vb = POOL_WIDTH // LANES, (POOL_WIDTH + DIFF_WIDTH) // LANES, (POOL_WIDTH + 2 * DIFF_WIDTH) // LANES
    row = pltpu.VMEM((1, tq), F32)
    acc = pltpu.VMEM((DIFF_V_DIM, tq), F32)
    sbuf = pltpu.VMEM((2, KT, tq), F32)
    tbuf = pltpu.VMEM((2, 1, tq), F32)
    return pl.pallas_call(
        functools.partial(_attn_kernel, tq=tq, lambda_init=lambda_init),
        grid=(b, DIFF_HEADS, l // tq),
        in_specs=[pl.BlockSpec((4, DIFF_QK_DIM), lambda bi, h, i: (0, 0)),
                  pl.BlockSpec((None, tq, LANES), lambda bi, h, i: (bi, i, qb + h)),
                  pl.BlockSpec((None, l, LANES), lambda bi, h, i: (bi, 0, kb + h)),
                  pl.BlockSpec((None, l, LANES), lambda bi, h, i: (bi, 0, vb + h)),
                  pl.BlockSpec((DIFF_V_DIM, 1), lambda bi, h, i: (0, 0))],
        out_specs=pl.BlockSpec((None, tq, DIFF_V_DIM), lambda bi, h, i: (bi, i, h)),
        out_shape=jax.ShapeDtypeStruct((b, l, DIFF_WIDTH), BF16),
        scratch_shapes=[pltpu.VMEM((l + LANES, LANES), BF16), pltpu.VMEM((l + LANES, LANES), BF16),
                        pltpu.VMEM((l // LANES + 1, DIFF_V_DIM, LANES), BF16),
                        sbuf, sbuf, tbuf, tbuf, row, row, row, row, acc, acc],
        compiler_params=_cparams(("parallel", "parallel", "arbitrary")),
        name="diff_attn",
    )(lam_params, z, z, z, subln_col)


def _gla_kernel(q_ref, k_ref, v_ref, r_ref, glr_ref, w2_ref, gb_ref, hn_ref, o_ref, st_ref, *, c_len, sb):
    c = pl.program_id(1)

    @pl.when(c == 0)
    def _():
        st_ref[...] = jnp.zeros(st_ref.shape, F32)

    pos = c * c_len + lax.broadcasted_iota(jnp.int32, (c_len, 1), 0)
    validf = (pos >= FIRST_VALID).astype(F32)
    ri = lax.broadcasted_iota(jnp.int32, (c_len, c_len), 0)
    ci = lax.broadcasted_iota(jnp.int32, (c_len, c_len), 1)
    tri = jnp.where(ri >= ci, 1.0, 0.0).astype(BF16)
    row = lax.broadcasted_iota(jnp.int32, (c_len, 1), 0)
    t_loc = lax.broadcasted_iota(jnp.int32, (sb, 1), 0)
    lane_c = lax.broadcasted_iota(jnp.int32, (sb, c_len), 1)
    glr = glr_ref[...]
    nt = (((1,), (1,)), ((), ()))
    tn = (((0,), (0,)), ((), ()))

    for h in range(GLA_HEADS):
        ks = slice(h * GLA_DK, (h + 1) * GLA_DK)
        vs = slice(h * GLA_DV, (h + 1) * GLA_DV)
        g = jnp.dot(glr, w2_ref[:, ks], preferred_element_type=F32) + gb_ref[:, ks]
        log_a = -(jnp.maximum(-g, 0.0) + jnp.log(1.0 + jnp.exp(-jnp.abs(g)))) * (1.0 / GLA_TAU)
        la_hi = log_a.astype(BF16)
        la_lo = (log_a - la_hi.astype(F32)).astype(BF16)
        b = (jnp.dot(tri, la_hi, preferred_element_type=F32)
             + jnp.dot(tri, la_lo, preferred_element_type=F32))
        q = q_ref[:, ks].astype(F32) * (GLA_DK ** -0.5)
        k = k_ref[:, ks].astype(F32) * validf
        v = v_ref[:, vs]
        st = st_ref[h]

        blocks = []
        for i in range(c_len // sb):
            lo = i * sb
            q_i = q[lo:lo + sb]
            b_i = b[lo:lo + sb]
            a_i = jnp.zeros((sb, c_len), F32)
            if i > 0:
                b_ref_row = b[lo - 1:lo]
                q_t = q_i * jnp.exp(b_i - b_ref_row)
                k_t = jnp.where(row < lo, k * jnp.exp(jnp.minimum(b_ref_row - b, 0.0)), 0.0)
                a_i = lax.dot_general(q_t.astype(BF16), k_t.astype(BF16), nt, preferred_element_type=F32)
            for s in range(sb):
                r = lo + s
                e = jnp.exp(jnp.minimum(b_i - b[r:r + 1], 0.0))
                col = jnp.sum(q_i * k[r:r + 1] * e, axis=-1, keepdims=True)
                col = jnp.where(t_loc >= s, col, 0.0)
                a_i = jnp.where(lane_c == r, col, a_i)
            blocks.append(a_i)
        att = jnp.concatenate(blocks, axis=0)

        o = jnp.dot(att.astype(BF16), v, preferred_element_type=F32)
        o = o + lax.dot_general((q * jnp.exp(b)).astype(BF16), st.astype(BF16), nt,
                                preferred_element_type=F32)
        b_last = b[c_len - 1:c_len]
        k_hat = (k * jnp.exp(b_last - b)).astype(BF16)
        st_ref[h] = st * jnp.exp(b_last) + lax.dot_general(v, k_hat, tn, preferred_element_type=F32)

        o = o * lax.rsqrt(jnp.mean(o * o, axis=-1, keepdims=True) + 1e-6) * hn_ref[...]
        rg = r_ref[:, vs].astype(F32)
        o_ref[:, vs] = (o * (rg * jax.nn.sigmoid(rg))).astype(o_ref.dtype)


def _gla(z, w2, gate_b, head_norm):
    b, l, _ = z.shape
    c_len = 64
    qw, vw = GLA_QK_WIDTH, GLA_V_WIDTH
    return pl.pallas_call(
        functools.partial(_gla_kernel, c_len=c_len, sb=16),
        grid=(b, l // c_len),
        in_specs=[pl.BlockSpec((None, c_len, qw), lambda bi, c: (bi, c, 0)),
                  pl.BlockSpec((None, c_len, qw), lambda bi, c: (bi, c, 1)),
                  pl.BlockSpec((None, c_len, vw), lambda bi, c: (bi, c, 1)),
                  pl.BlockSpec((None, c_len, vw), lambda bi, c: (bi, c, 2)),
                  pl.BlockSpec((None, c_len, LANES), lambda bi, c: (bi, c, (2 * qw + 2 * vw) // LANES)),
                  pl.BlockSpec((LANES, qw), lambda bi, c: (0, 0)),
                  pl.BlockSpec((1, qw), lambda bi, c: (0, 0)),
                  pl.BlockSpec((1, GLA_DV), lambda bi, c: (0, 0))],
        out_specs=pl.BlockSpec((None, c_len, vw), lambda bi, c: (bi, c, 0)),
        out_shape=jax.ShapeDtypeStruct((b, l, vw), BF16),
        scratch_shapes=[pltpu.VMEM((GLA_HEADS, GLA_DV, GLA_DK), F32)],
        compiler_params=_cparams(("parallel", "arbitrary")),
        name="gla_mixer",
    )(z, z, z, z, z, w2, gate_b, head_norm)


def _mixout_kernel(*refs):
    *y_refs, w_ref, h_ref, g_ref, b_ref, o_ref = refs
    acc = ALPHA * h_ref[...]
    off = 0
    for y_ref in y_refs:
        kd = y_ref.shape[1]
        acc = acc + jnp.dot(y_ref[...], w_ref[off:off + kd, :], preferred_element_type=F32)
        off += kd
    o_ref[...] = _layer_norm(acc, g_ref[...], b_ref[...])


def _mixout(ys, w, h2d, g, b, name):
    n, d = h2d.shape
    tm = _pick(n, (640, 512, 256, 128))
    return pl.pallas_call(
        _mixout_kernel,
        grid=(n // tm,),
        in_specs=[pl.BlockSpec((tm, y.shape[1]), lambda i: (i, 0)) for y in ys]
        + [pl.BlockSpec(w.shape, lambda i: (0, 0)),
           pl.BlockSpec((tm, d), lambda i: (i, 0)),
           pl.BlockSpec((1, d), lambda i: (0, 0)),
           pl.BlockSpec((1, d), lambda i: (0, 0))],
        out_specs=pl.BlockSpec((tm, d), lambda i: (i, 0)),
        out_shape=jax.ShapeDtypeStruct((n, d), F32),
        compiler_params=_cparams(("parallel",)),
        name=name,
    )(*ys, w, h2d, g, b)


def _router_kernel(bias_ref, h_ref, rw_ref, haug_ref, pos_ref, cnt_ref, upper_scr, cnt_scr, *, cap):
    tm, d = h_ref.shape

    @pl.when(pl.program_id(0) == 0)
    def _():
        r = lax.broadcasted_iota(jnp.int32, (tm, tm), 0)
        c = lax.broadcasted_iota(jnp.int32, (tm, tm), 1)
        upper_scr[...] = jnp.where(r < c, 1.0, 0.0).astype(BF16)
        cnt_scr[...] = jnp.zeros(cnt_scr.shape, F32)

    x = h_ref[...]
    w = rw_ref[...]
    xh = x.astype(BF16)
    xl = (x - xh.astype(F32)).astype(BF16)
    wh = w.astype(BF16)
    wl = (w - wh.astype(F32)).astype(BF16)
    nt = (((1,), (1,)), ((), ()))
    logits = (lax.dot_general(wh, xh, nt, preferred_element_type=F32)
              + lax.dot_general(wh, xl, nt, preferred_element_type=F32)
              + lax.dot_general(wl, xh, nt, preferred_element_type=F32))
    lg = [logits[e:e + 1, :] for e in range(N_EXPERTS)]
    mx = functools.reduce(jnp.maximum, lg)
    ex = [jnp.exp(v - mx) for v in lg]
    den = functools.reduce(jnp.add, ex)
    probs = [v / den for v in ex]
    sel = [probs[e] + bias_ref[e] for e in range(N_EXPERTS)]

    def top2_sum(a, b, c, d_):
        hi1, lo1 = jnp.maximum(a, b), jnp.minimum(a, b)
        hi2, lo2 = jnp.maximum(c, d_), jnp.minimum(c, d_)
        return jnp.maximum(hi1, hi2) + jnp.maximum(jnp.minimum(hi1, hi2), jnp.maximum(lo1, lo2))

    best = top2_sum(*sel[0:4])
    gidx = jnp.zeros((1, tm), jnp.int32)
    for g in range(1, N_GROUPS):
        sc = top2_sum(*sel[4 * g:4 * g + 4])
        better = sc > best
        gidx = jnp.where(better, g, gidx)
        best = jnp.maximum(best, sc)

    def pick(vals, j):
        out = vals[j]
        for g in range(1, N_GROUPS):
            out = jnp.where(gidx == g, vals[4 * g + j], out)
        return out

    sg = [pick(sel, j) for j in range(EXPERTS_PER_GROUP)]
    pg = [pick(probs, j) for j in range(EXPERTS_PER_GROUP)]
    chosen = []
    for j in range(EXPERTS_PER_GROUP):
        rank = jnp.zeros((1, tm), jnp.int32)
        for o in range(EXPERTS_PER_GROUP):
            if o == j:
                continue
            ahead = (sg[o] > sg[j]) if o > j else (sg[o] >= sg[j])
            rank = rank + ahead.astype(jnp.int32)
        chosen.append(jnp.where(rank < 2, pg[j], 0.0))
    wsum = functools.reduce(jnp.add, chosen)
    wn = [cj / wsum for cj in chosen]

    onehot = [jnp.where(gidx == g, 1.0, 0.0) for g in range(N_GROUPS)]
    oh8 = jnp.concatenate(onehot + [jnp.zeros((8 - N_GROUPS, tm), F32)], axis=0)
    before = jnp.dot(oh8.astype(BF16), upper_scr[...], preferred_element_type=F32)
    seen = before + cnt_scr[:, 0:1]
    rank_in_group = functools.reduce(jnp.add, [onehot[g] * seen[g:g + 1, :] for g in range(N_GROUPS)])
    pos_ref[...] = gidx * cap + rank_in_group.astype(jnp.int32)
    cnt_scr[...] = cnt_scr[...] + jnp.sum(oh8, axis=1, keepdims=True)
    cnt_ref[...] = cnt_scr[...]

    pay_t = jnp.concatenate(wn + [jnp.zeros((LANES - EXPERTS_PER_GROUP, tm), F32)], axis=0)
    haug_ref[:, 0:d] = x
    haug_ref[:, d:] = pay_t.T


def _router(h2d, router_w_t, router_bias, cap):
    n, d = h2d.shape
    tm = _pick(n, (640, 512, 256, 128))
    return pl.pallas_call(
        functools.partial(_router_kernel, cap=cap),
        grid=(n // tm,),
        in_specs=[pl.BlockSpec(memory_space=pltpu.SMEM),
                  pl.BlockSpec((tm, d), lambda i: (i, 0)),
                  pl.BlockSpec((N_EXPERTS, d), lambda i: (0, 0))],
        out_specs=[pl.BlockSpec((tm, d + LANES), lambda i: (i, 0)),
                   pl.BlockSpec((None, 1, tm), lambda i: (i, 0, 0)),
                   pl.BlockSpec((8, LANES), lambda i: (0, 0))],
        out_shape=[jax.ShapeDtypeStruct((n, d + LANES), F32),
                   jax.ShapeDtypeStruct((n // tm, 1, tm), jnp.int32),
                   jax.ShapeDtypeStruct((8, LANES), F32)],
        scratch_shapes=[pltpu.VMEM((tm, tm), BF16), pltpu.VMEM((8, LANES), F32)],
        compiler_params=_cparams(("arbitrary",)),
        name="router",
    )(router_bias, h2d, router_w_t)


def _row_dma(src_ref, dst_ref, sem, src_row, dst_row):
    return pltpu.make_async_copy(src_ref.at[pl.ds(src_row, 1), :], dst_ref.at[pl.ds(dst_row, 1), :], sem)


def _scatter_rows_kernel(idx_ref, src_ref, dst_ref, sem, *, rows):
    base = pl.program_id(0) * rows

    def issue(r, c):
        _row_dma(src_ref, dst_ref, sem, r, idx_ref[base + r]).start()
        return c

    def drain(r, c):
        _row_dma(src_ref, dst_ref, sem, 0, 0).wait()
        return c

    lax.fori_loop(0, rows, issue, 0, unroll=8)
    lax.fori_loop(0, rows, drain, 0, unroll=8)


def _gather_rows_kernel(idx_ref, src_ref, dst_ref, sem, *, rows):
    base = pl.program_id(0) * rows

    def issue(r, c):
        _row_dma(src_ref, dst_ref, sem, idx_ref[base + r], r).start()
        return c

    def drain(r, c):
        _row_dma(src_ref, dst_ref, sem, 0, 0).wait()
        return c

    lax.fori_loop(0, rows, issue, 0, unroll=8)
    lax.fori_loop(0, rows, drain, 0, unroll=8)


def _permute_rows(body, idx, src, n_dst, name):
    m = idx.shape[0]
    width = src.shape[1]
    rows = _pick(m, (512, 256, 128))
    tile = pl.BlockSpec((rows, width), lambda i, idx_ref: (i, 0))
    whole = pl.BlockSpec(memory_space=pl.ANY)
    scatter = body is _scatter_rows_kernel
    return pl.pallas_call(
        functools.partial(body, rows=rows),
        grid_spec=pltpu.PrefetchScalarGridSpec(
            num_scalar_prefetch=1,
            grid=(m // rows,),
            in_specs=[tile if scatter else whole],
            out_specs=whole if scatter else tile,
            scratch_shapes=[pltpu.SemaphoreType.DMA(())]),
        out_shape=jax.ShapeDtypeStruct((n_dst, width), src.dtype),
        compiler_params=_cparams(("arbitrary",)),
        name=name,
    )(idx, src)


def _moe_kernel(grp_ref, blk_ref, nv_ref, x_ref, wg_ref, wu_ref, wd_ref, g_ref, b_ref, o_ref):
    nv = nv_ref[pl.program_id(0)]

    @pl.when(nv > 0)
    def _():
        tm, d = o_ref.shape
        live = lax.broadcasted_iota(jnp.int32, (tm, 1), 0) < nv
        x = jnp.where(live, x_ref[:, 0:d], 0.0)
        cw = jnp.where(live, x_ref[:, d:], 0.0)
        xb = x.astype(BF16)
        acc = ALPHA * x
        for j in range(EXPERTS_PER_GROUP):
            gate = jnp.dot(xb, wg_ref[j], preferred_element_type=F32)
            up = jnp.dot(xb, wu_ref[j], preferred_element_type=F32)
            act = gate * jax.nn.sigmoid(gate) * up * cw[:, j:j + 1]
            acc = acc + jnp.dot(act.astype(BF16), wd_ref[j], preferred_element_type=F32)
        o_ref[...] = _layer_norm(acc, g_ref[...], b_ref[...])


def _moe(x_sorted, tables, wg, wu, wd, g, b, tm, name):
    rows, wa = x_sorted.shape
    d = wa - LANES
    n_tiles = tables[0].shape[0]
    epg = EXPERTS_PER_GROUP
    return pl.pallas_call(
        _moe_kernel,
        grid_spec=pltpu.PrefetchScalarGridSpec(
            num_scalar_prefetch=3,
            grid=(n_tiles,),
            in_specs=[pl.BlockSpec((tm, wa), lambda t, grp, blk, nv: (blk[t], 0)),
                      pl.BlockSpec((epg, d, D_EXPERT), lambda t, grp, blk, nv: (grp[t], 0, 0)),
                      pl.BlockSpec((epg, d, D_EXPERT), lambda t, grp, blk, nv: (grp[t], 0, 0)),
                      pl.BlockSpec((epg, D_EXPERT, d), lambda t, grp, blk, nv: (grp[t], 0, 0)),
                      pl.BlockSpec((1, d), lambda t, grp, blk, nv: (0, 0)),
                      pl.BlockSpec((1, d), lambda t, grp, blk, nv: (0, 0))],
            out_specs=pl.BlockSpec((tm, d), lambda t, grp, blk, nv: (blk[t], 0))),
        out_shape=jax.ShapeDtypeStruct((rows, d), F32),
        compiler_params=_cparams(("arbitrary",)),
        name=name,
    )(*tables, x_sorted, wg, wu, wd, g, b)


def _tile_tables(counts, tm, cap, n_tiles):
    tiles_g = (counts + tm - 1) // tm
    ends = jnp.cumsum(tiles_g)
    starts = ends - tiles_g
    t = jnp.arange(n_tiles, dtype=jnp.int32)
    tc = jnp.minimum(t, ends[-1] - 1)
    grp = jnp.sum((tc[:, None] >= ends[None, :]).astype(jnp.int32), axis=1)
    local = tc - starts[grp]
    blk = grp * (cap // tm) + local
    nv = jnp.where(t < ends[-1], jnp.clip(counts[grp] - local * tm, 0, tm), 0)
    return grp.astype(jnp.int32), blk.astype(jnp.int32), nv.astype(jnp.int32)


def _routed_moe(h2d, router_w_t, router_bias, wg, wu, wd, g, b, out_idx_fn, n_out, name):
    n, d = h2d.shape
    cap = n
    tm = _pick(n, (512, 256, 128))
    haug, pos, cnt = _router(h2d, router_w_t, router_bias, cap)
    pos = pos.reshape(n)
    counts = cnt[:N_GROUPS, 0].astype(jnp.int32)
    x_sorted = _permute_rows(_scatter_rows_kernel, pos, haug, N_GROUPS * cap, name + "_scatter")
    tables = _tile_tables(counts, tm, cap, n // tm + N_GROUPS)
    y_sorted = _moe(x_sorted, tables, wg, wu, wd, g, b, tm, name)
    return _permute_rows(_gather_rows_kernel, out_idx_fn(pos), y_sorted, n_out, name + "_gather")


def kernel(x, meta, even_w_in, pool_w, pool_scale, diff_lq1, diff_lk1, diff_lq2, diff_lk2, diff_subln,
           even_w_out, odd_w_in, gla_gate_w2, gla_gate_b, gla_head_norm, odd_w_out, ln_mix_g, ln_mix_b,
           ln_ffn_g, ln_ffn_b, router_w, router_bias, moe_w_gate, moe_w_up, moe_w_down):
    bsz, seq, d = x.shape
    l = PREFIX + seq
    n = bsz * l
    pad = jnp.zeros((bsz, FIRST_VALID, d), x.dtype)
    metas = jnp.broadcast_to(meta.astype(x.dtype)[None], (bsz, N_META, d))
    h = jnp.concatenate([pad, metas, x], axis=1).reshape(n, d)
    router_w_t = router_w.T
    row = lambda a: a.reshape(1, -1)

    for i in range(DEPTH):
        j = i // 2
        if i % 2 == 0:
            lambda_init = 0.8 - 0.6 * math.exp(-0.3 * i)
            z = _proj(h, even_w_in[j].astype(BF16), "even_in_proj").reshape(bsz, l, EVEN_IN)
            y_pool = _pool(z, pool_w[j].astype(BF16), row(pool_scale[j]))
            lam_params = jnp.stack([diff_lq1[j], diff_lk1[j], diff_lq2[j], diff_lk2[j]])
            y_attn = _attn(z, lam_params, diff_subln[j].reshape(-1, 1), lambda_init)
            ys = [y_pool.reshape(n, POOL_WIDTH), y_attn.reshape(n, DIFF_WIDTH)]
            h = _mixout(ys, even_w_out[j].astype(BF16), h, row(ln_mix_g[i]), row(ln_mix_b[i]), "even_out_proj")
        else:
            w_in = jnp.pad(odd_w_in[j], ((0, 0), (0, ODD_IN_PAD - ODD_IN))).astype(BF16)
            z = _proj(h, w_in, "odd_in_proj").reshape(bsz, l, ODD_IN_PAD)
            w2 = jnp.pad(gla_gate_w2[j], ((0, LANES - GLA_RANK), (0, 0))).astype(BF16)
            y = _gla(z, w2, row(gla_gate_b[j]), row(gla_head_norm[j]))
            h = _mixout([y.reshape(n, GLA_V_WIDTH)], odd_w_out[j].astype(BF16), h,
                        row(ln_mix_g[i]), row(ln_mix_b[i]), "odd_out_proj")
        last = i == DEPTH - 1
        if last:
            out_idx_fn = lambda pos: pos.reshape(bsz, l)[:, PREFIX:].reshape(-1)
        else:
            out_idx_fn = lambda pos: pos
        h = _routed_moe(h, router_w_t, router_bias, moe_w_gate[i].astype(BF16), moe_w_up[i].astype(BF16),
                        moe_w_down[i].astype(BF16), row(ln_ffn_g[i]), row(ln_ffn_b[i]),
                        out_idx_fn, bsz * seq if last else n, f"moe_{i}")
    return h.reshape(bsz, seq, d)
```

```python
import functools
import math

import jax
import jax.numpy as jnp
from jax import lax
from jax.experimental import pallas as pl
from jax.experimental.pallas import tpu as pltpu

F32 = jnp.float32
BF16 = jnp.bfloat16

D_MODEL = 1024
DEPTH = 2
N_META = 16
PREFIX = 128
FIRST_VALID = PREFIX - N_META
POOL_WINDOWS = (2, 4, 8, 16)
POOL_GROUP = 128
POOL_WIDTH = 512
MAX_WINDOW = 16
DIFF_HEADS = 4
DIFF_QK_DIM = 64
DIFF_V_DIM = 128
DIFF_WIDTH = 512
EVEN_IN = 2048
GLA_HEADS = 4
GLA_DK = 128
GLA_DV = 256
GLA_RANK = 16
GLA_TAU = 16.0
GLA_QK_WIDTH = GLA_HEADS * GLA_DK
GLA_V_WIDTH = GLA_HEADS * GLA_DV
ODD_IN = 2 * GLA_QK_WIDTH + 2 * GLA_V_WIDTH + GLA_RANK
ODD_IN_PAD = 3200
N_EXPERTS = 16
N_GROUPS = 4
EXPERTS_PER_GROUP = 4
D_EXPERT = 512
TOP_K = 2
PAIRS = ((0, 1), (0, 2), (0, 3), (1, 2), (1, 3), (2, 3))
N_BUCKETS = N_GROUPS * len(PAIRS)
BUCKET_ROWS = 32
ALPHA = (2.0 * DEPTH) ** 0.25
NEG_BIG = -1e30
KT = 512

LANES = 128
VMEM_LIMIT = 56 * 1024 * 1024


def _pick(n, candidates):
    for c in candidates:
        if n % c == 0:
            return c
    raise ValueError(f"no tile for {n} in {candidates}")


def _cparams(sem):
    return pltpu.CompilerParams(dimension_semantics=sem, vmem_limit_bytes=VMEM_LIMIT)


def _layer_norm(a, g, b):
    mu = jnp.mean(a, axis=-1, keepdims=True)
    d = a - mu
    var = jnp.mean(d * d, axis=-1, keepdims=True)
    return d * lax.rsqrt(var + 1e-5) * g + b


def _proj_kernel(x_ref, w_ref, o_ref, *, tn):
    xb = x_ref[...].astype(BF16)
    for j in range(o_ref.shape[1] // tn):
        cols = slice(j * tn, (j + 1) * tn)
        o_ref[:, cols] = jnp.dot(xb, w_ref[:, cols], preferred_element_type=F32).astype(o_ref.dtype)


def _proj(x2d, w, name):
    n, k = x2d.shape
    wout = w.shape[1]
    tm = _pick(n, (640, 512, 256, 128))
    tn = _pick(wout, (640, 512, 128))
    return pl.pallas_call(
        functools.partial(_proj_kernel, tn=tn),
        grid=(n // tm,),
        in_specs=[pl.BlockSpec((tm, k), lambda i: (i, 0)),
                  pl.BlockSpec((k, wout), lambda i: (0, 0))],
        out_specs=pl.BlockSpec((tm, wout), lambda i: (i, 0)),
        out_shape=jax.ShapeDtypeStruct((n, wout), BF16),
        compiler_params=_cparams(("parallel",)),
        name=name,
    )(x2d, w)


def _pool_kernel(cur_ref, halo_ref, w_ref, sc_ref, o_ref, u_scr):
    t = pl.program_id(1)
    tl = cur_ref.shape[0]
    pos = t * tl + lax.broadcasted_iota(jnp.int32, (tl, 1), 0)
    hpos = t * tl - MAX_WINDOW + lax.broadcasted_iota(jnp.int32, (MAX_WINDOW, 1), 0)
    u_scr[0:MAX_WINDOW, :] = jnp.where(hpos >= FIRST_VALID, halo_ref[...].astype(F32), 0.0)
    u_scr[MAX_WINDOW:, :] = jnp.where(pos >= FIRST_VALID, cur_ref[...].astype(F32), 0.0)
    n_valid = pos - (FIRST_VALID - 1)
    for gi, w in enumerate(POOL_WINDOWS):
        cols = slice(gi * POOL_GROUP, (gi + 1) * POOL_GROUP)
        u = u_scr[MAX_WINDOW:MAX_WINDOW + tl, cols]
        win = u
        for j in range(1, w):
            win = win + u_scr[MAX_WINDOW - j:MAX_WINDOW - j + tl, cols]
        cnt = jnp.clip(n_valid, 1, w).astype(F32)
        d = win / cnt - u
        y = jnp.dot(d.astype(BF16), w_ref[gi], preferred_element_type=F32) * sc_ref[:, cols]
        o_ref[:, cols] = y.astype(o_ref.dtype)


def _pool(z, pool_w, pool_scale):
    b, l, _ = z.shape
    tl = _pick(l, (640, 128))
    hb = tl // MAX_WINDOW
    return pl.pallas_call(
        _pool_kernel,
        grid=(b, l // tl),
        in_specs=[pl.BlockSpec((None, tl, POOL_WIDTH), lambda bi, t: (bi, t, 0)),
                  pl.BlockSpec((None, MAX_WINDOW, POOL_WIDTH),
                               lambda bi, t: (bi, jnp.maximum(t * hb - 1, 0), 0)),
                  pl.BlockSpec((len(POOL_WINDOWS), POOL_GROUP, POOL_GROUP), lambda bi, t: (0, 0, 0)),
                  pl.BlockSpec((1, POOL_WIDTH), lambda bi, t: (0, 0))],
        out_specs=pl.BlockSpec((None, tl, POOL_WIDTH), lambda bi, t: (bi, t, 0)),
        out_shape=jax.ShapeDtypeStruct((b, l, POOL_WIDTH), BF16),
        scratch_shapes=[pltpu.VMEM((tl + MAX_WINDOW, POOL_WIDTH), F32)],
        compiler_params=_cparams(("parallel", "parallel")),
        name="pool_mixer",
    )(z, z, pool_w, pool_scale)


def _attn_kernel(lam_ref, q_ref, k_ref, v_ref, sub_ref, o_ref, ka_scr, kb_scr, vt_scr,
                 s0_scr, s1_scr, t0_scr, t1_scr, m0_scr, m1_scr, l0_scr, l1_scr, acc0_scr, acc1_scr,
                 *, tq, lambda_init):
    h = pl.program_id(1)
    i = pl.program_id(2)
    slope = jnp.exp2(-2.0 * (h + 1).astype(F32))
    half = DIFF_QK_DIM
    n_chunks = k_ref.shape[0] // LANES
    n_tail, n_pad = _attn_tail(tq)

    @pl.when(i == 0)
    def _():
        lane = lax.broadcasted_iota(jnp.int32, (LANES, 2 * half), 1)
        rowf = lax.broadcasted_iota(jnp.int32, (LANES, 2 * half), 0).astype(F32)

        def build(c, carry):
            rows = pl.ds(pl.multiple_of(c * LANES, LANES), LANES)
            kt = k_ref[rows, :].astype(F32)
            hi = jnp.full((LANES, 2 * half), c, jnp.int32).astype(F32)
            ka = jnp.where(lane < half, kt, jnp.where(lane == half, hi, jnp.where(lane == half + 1, rowf, 0.0)))
            kb = jnp.where(lane >= half, kt, jnp.where(lane == 0, hi, jnp.where(lane == 1, rowf, 0.0)))
            ka_scr[rows, :] = ka.astype(BF16)
            kb_scr[rows, :] = kb.astype(BF16)
            vt_scr[c] = v_ref[rows, :].astype(F32).T.astype(BF16)
            return carry

        lax.fori_loop(0, n_chunks, build, 0)
        for extra in range(n_pad):
            pad_rows = pl.ds((n_chunks + extra) * LANES, LANES)
            ka_scr[pad_rows, :] = jnp.zeros((LANES, 2 * half), BF16)
            kb_scr[pad_rows, :] = jnp.zeros((LANES, 2 * half), BF16)
            vt_scr[n_chunks + extra] = jnp.zeros((DIFF_V_DIM, LANES), BF16)

    lane = lax.broadcasted_iota(jnp.int32, (tq, 2 * half), 1)
    q = (q_ref[...] * jnp.asarray(half ** -0.5, BF16)).astype(F32)
    f_hi = LANES * slope
    qa = jnp.where(lane < half, q, jnp.where(lane == half, f_hi, jnp.where(lane == half + 1, slope, 0.0)))
    qb = jnp.where(lane >= half, q, jnp.where(lane == 0, f_hi, jnp.where(lane == 1, slope, 0.0)))
    q_maps = (qa.T.astype(BF16), qb.T.astype(BF16))
    k_maps = (ka_scr, kb_scr)
    s_scrs, t_scrs = (s0_scr, s1_scr), (t0_scr, t1_scr)
    m_scrs, l_scrs, acc_scrs = (m0_scr, m1_scr), (l0_scr, l1_scr), (acc0_scr, acc1_scr)
    for mp in range(2):
        m_scrs[mp][...] = jnp.full(m0_scr.shape, NEG_BIG, F32)
        l_scrs[mp][...] = jnp.zeros(l0_scr.shape, F32)
        acc_scrs[mp][...] = jnp.zeros(acc0_scr.shape, F32)
    q0 = i * tq
    n_full = q0 // KT

    def scores(t, mode):
        k0 = pl.multiple_of(t * KT, KT)
        buf = t % 2
        allowed = None
        if mode is not None:
            kpos = k0 + lax.broadcasted_iota(jnp.int32, (KT, tq), 0)
            allowed = kpos >= FIRST_VALID
            if mode == "diag":
                qpos = q0 + lax.broadcasted_iota(jnp.int32, (KT, tq), 1)
                allowed = jnp.logical_and(kpos <= qpos, jnp.logical_or(allowed, kpos == qpos))
        for mp in range(2):
            s = jnp.dot(k_maps[mp][pl.ds(k0, KT), :], q_maps[mp], preferred_element_type=F32)
            if allowed is not None:
                s = jnp.where(allowed, s, NEG_BIG)
            s_scrs[mp][buf] = s
            t_scrs[mp][buf] = jnp.max(s, axis=0, keepdims=True)

    def accumulate(t):
        c0 = t * (KT // LANES)
        buf = t % 2
        vt = jnp.concatenate([vt_scr[c0 + u] for u in range(KT // LANES)], axis=1)
        for mp in range(2):
            m_prev = m_scrs[mp][...]
            m_new = jnp.maximum(m_prev, t_scrs[mp][buf])
            alpha = jnp.exp(m_prev - m_new)
            p = jnp.exp(s_scrs[mp][buf] - m_new)
            l_scrs[mp][...] = alpha * l_scrs[mp][...] + jnp.sum(p, axis=0, keepdims=True)
            acc_scrs[mp][...] = alpha * acc_scrs[mp][...] + jnp.dot(vt, p.astype(BF16), preferred_element_type=F32)
            m_scrs[mp][...] = m_new

    @pl.when(n_full >= 1)
    def _():
        scores(0, "valid")

    @pl.when(n_full == 0)
    def _():
        scores(0, "diag")

    def steady(t, c):
        accumulate(t)
        scores(t + 1, None)
        return c

    lax.fori_loop(0, n_full - 1, steady, 0)

    @pl.when(n_full >= 1)
    def _():
        accumulate(n_full - 1)
        scores(n_full, "diag")

    for u in range(n_tail - 1):
        accumulate(n_full + u)
        scores(n_full + u + 1, "diag")
    accumulate(n_full + n_tail - 1)

    lam = (jnp.exp(jnp.sum(lam_ref[0:1, :] * lam_ref[1:2, :], axis=-1, keepdims=True))
           - jnp.exp(jnp.sum(lam_ref[2:3, :] * lam_ref[3:4, :], axis=-1, keepdims=True)) + lambda_init)
    o = acc0_scr[...] / l0_scr[...] - lam * (acc1_scr[...] / l1_scr[...])
    o = o * lax.rsqrt(jnp.mean(o * o, axis=0, keepdims=True) + 1e-6) * sub_ref[...] * (1.0 - lambda_init)
    o_ref[...] = o.T.astype(o_ref.dtype)


def _attn_tail(tq):
    n_tail = -(-(tq + KT - LANES) // KT)
    return n_tail, (n_tail * KT - tq) // LANES


def _attn(z, lam_params, subln_col, lambda_init):
    b, l, _ = z.shape
    tq = 5 * LANES
    assert l % tq == 0 and tq % LANES == 0 and KT % LANES == 0
    n_pad = _attn_tail(tq)[1]
    qb, kb, vb = POOL_WIDTH // LANES, (POOL_WIDTH + DIFF_WIDTH) // LANES, (POOL_WIDTH + 2 * DIFF_WIDTH) // LANES
    row = pltpu.VMEM((1, tq), F32)
    acc = pltpu.VMEM((DIFF_V_DIM, tq), F32)
    sbuf = pltpu.VMEM((2, KT, tq), F32)
    tbuf = pltpu.VMEM((2, 1, tq), F32)
    return pl.pallas_call(
        functools.partial(_attn_kernel, tq=tq, lambda_init=lambda_init),
        grid=(b, DIFF_HEADS, l // tq),
        in_specs=[pl.BlockSpec((4, DIFF_QK_DIM), lambda bi, h, i: (0, 0)),
                  pl.BlockSpec((None, tq, LANES), lambda bi, h, i: (bi, i, qb + h)),
                  pl.BlockSpec((None, l, LANES), lambda bi, h, i: (bi, 0, kb + h)),
                  pl.BlockSpec((None, l, LANES), lambda bi, h, i: (bi, 0, vb + h)),
                  pl.BlockSpec((DIFF_V_DIM, 1), lambda bi, h, i: (0, 0))],
        out_specs=pl.BlockSpec((None, tq, DIFF_V_DIM), lambda bi, h, i: (bi, i, h)),
        out_shape=jax.ShapeDtypeStruct((b, l, DIFF_WIDTH), BF16),
        scratch_shapes=[pltpu.VMEM((l + n_pad * LANES, LANES), BF16), pltpu.VMEM((l + n_pad * LANES, LANES), BF16),
                        pltpu.VMEM((l // LANES + n_pad, DIFF_V_DIM, LANES), BF16),
                        sbuf, sbuf, tbuf, tbuf, row, row, row, row, acc, acc],
        compiler_params=_cparams(("parallel", "parallel", "arbitrary")),
        name="diff_attn",
    )(lam_params, z, z, z, subln_col)


def _gla_kernel(q_ref, k_ref, v_ref, r_ref, glr_ref, w2_ref, gb_ref, hn_ref, o_ref, st_ref, *, c_len, sb):
    c = pl.program_id(0)
    n_batch = q_ref.shape[0]

    @pl.when(c == 0)
    def _():
        st_ref[...] = jnp.zeros(st_ref.shape, F32)

    pos = c * c_len + lax.broadcasted_iota(jnp.int32, (c_len, 1), 0)
    validf = (pos >= FIRST_VALID).astype(F32)
    ri = lax.broadcasted_iota(jnp.int32, (c_len, c_len), 0)
    ci = lax.broadcasted_iota(jnp.int32, (c_len, c_len), 1)
    tri = jnp.where(ri >= ci, 1.0, 0.0).astype(BF16)
    row = lax.broadcasted_iota(jnp.int32, (c_len, 1), 0)
    t_loc = lax.broadcasted_iota(jnp.int32, (sb, 1), 0)
    lane_c = lax.broadcasted_iota(jnp.int32, (sb, c_len), 1)
    nt = (((1,), (1,)), ((), ()))
    tn = (((0,), (0,)), ((), ()))

    for bi, h in [(bi, h) for bi in range(n_batch) for h in range(GLA_HEADS)]:
        glr = glr_ref[bi]
        ks = slice(h * GLA_DK, (h + 1) * GLA_DK)
        vs = slice(h * GLA_DV, (h + 1) * GLA_DV)
        g = jnp.dot(glr, w2_ref[:, ks], preferred_element_type=F32) + gb_ref[:, ks]
        log_a = -(jnp.maximum(-g, 0.0) + jnp.log(1.0 + jnp.exp(-jnp.abs(g)))) * (1.0 / GLA_TAU)
        la_hi = log_a.astype(BF16)
        la_lo = (log_a - la_hi.astype(F32)).astype(BF16)
        b = (jnp.dot(tri, la_hi, preferred_element_type=F32)
             + jnp.dot(tri, la_lo, preferred_element_type=F32))
        q = q_ref[bi, :, ks].astype(F32) * (GLA_DK ** -0.5)
        k = k_ref[bi, :, ks].astype(F32) * validf
        v = v_ref[bi, :, vs]
        st = st_ref[bi, h]

        blocks = []
        for i in range(c_len // sb):
            lo = i * sb
            q_i = q[lo:lo + sb]
            b_i = b[lo:lo + sb]
            a_i = jnp.zeros((sb, c_len), F32)
            if i > 0:
                b_ref_row = b[lo - 1:lo]
                q_t = q_i * jnp.exp(b_i - b_ref_row)
                k_t = jnp.where(row < lo, k * jnp.exp(jnp.minimum(b_ref_row - b, 0.0)), 0.0)
                a_i = lax.dot_general(q_t.astype(BF16), k_t.astype(BF16), nt, preferred_element_type=F32)
            for s in range(sb):
                r = lo + s
                e = jnp.exp(b_i - b[r:r + 1])
                col = jnp.sum(q_i * k[r:r + 1] * e, axis=-1, keepdims=True)
                col = jnp.where(t_loc >= s, col, 0.0)
                a_i = jnp.where(lane_c == r, col, a_i)
            blocks.append(a_i)
        att = jnp.concatenate(blocks, axis=0)

        o = jnp.dot(att.astype(BF16), v, preferred_element_type=F32)
        o = o + lax.dot_general((q * jnp.exp(b)).astype(BF16), st.astype(BF16), nt,
                                preferred_element_type=F32)
        b_last = b[c_len - 1:c_len]
        k_hat = (k * jnp.exp(b_last - b)).astype(BF16)
        st_ref[bi, h] = st * jnp.exp(b_last) + lax.dot_general(v, k_hat, tn, preferred_element_type=F32)

        o = o * lax.rsqrt(jnp.mean(o * o, axis=-1, keepdims=True) + 1e-6) * hn_ref[...]
        rg = r_ref[bi, :, vs].astype(F32)
        o_ref[bi, :, vs] = (o * (rg * jax.nn.sigmoid(rg))).astype(o_ref.dtype)


def _gla(z, w2, gate_b, head_norm):
    b, l, _ = z.shape
    c_len = 64
    qw, vw = GLA_QK_WIDTH, GLA_V_WIDTH
    return pl.pallas_call(
        functools.partial(_gla_kernel, c_len=c_len, sb=16),
        grid=(l // c_len,),
        in_specs=[pl.BlockSpec((b, c_len, qw), lambda c: (0, c, 0)),
                  pl.BlockSpec((b, c_len, qw), lambda c: (0, c, 1)),
                  pl.BlockSpec((b, c_len, vw), lambda c: (0, c, 1)),
                  pl.BlockSpec((b, c_len, vw), lambda c: (0, c, 2)),
                  pl.BlockSpec((b, c_len, LANES), lambda c: (0, c, (2 * qw + 2 * vw) // LANES)),
                  pl.BlockSpec((LANES, qw), lambda c: (0, 0)),
                  pl.BlockSpec((1, qw), lambda c: (0, 0)),
                  pl.BlockSpec((1, GLA_DV), lambda c: (0, 0))],
        out_specs=pl.BlockSpec((b, c_len, vw), lambda c: (0, c, 0)),
        out_shape=jax.ShapeDtypeStruct((b, l, vw), BF16),
        scratch_shapes=[pltpu.VMEM((b, GLA_HEADS, GLA_DV, GLA_DK), F32)],
        compiler_params=_cparams(("arbitrary",)),
        name="gla_mixer",
    )(z, z, z, z, z, w2, gate_b, head_norm)


def _route_tokens(x, bias_ref, rw_ref, haug_ref, bucket_ref, rank_ref, cnt_ref, upper_scr, cnt_scr):
    tm, d = x.shape

    @pl.when(pl.program_id(0) == 0)
    def _():
        r = lax.broadcasted_iota(jnp.int32, (tm, tm), 0)
        c = lax.broadcasted_iota(jnp.int32, (tm, tm), 1)
        upper_scr[...] = jnp.where(r < c, 1.0, 0.0).astype(BF16)
        cnt_scr[...] = jnp.zeros(cnt_scr.shape, F32)

    w = rw_ref[...]
    xh = x.astype(BF16)
    xl = (x - xh.astype(F32)).astype(BF16)
    wh = w.astype(BF16)
    wl = (w - wh.astype(F32)).astype(BF16)
    nt = (((1,), (1,)), ((), ()))
    logits = (lax.dot_general(wh, xh, nt, preferred_element_type=F32)
              + lax.dot_general(wh, xl, nt, preferred_element_type=F32)
              + lax.dot_general(wl, xh, nt, preferred_element_type=F32))
    lg = [logits[e:e + 1, :] for e in range(N_EXPERTS)]
    mx = functools.reduce(jnp.maximum, lg)
    ex = [jnp.exp(v - mx) for v in lg]
    den = functools.reduce(jnp.add, ex)
    probs = [v / den for v in ex]
    sel = [probs[e] + bias_ref[e] for e in range(N_EXPERTS)]

    def top2_sum(a, b, c, d_):
        hi1, lo1 = jnp.maximum(a, b), jnp.minimum(a, b)
        hi2, lo2 = jnp.maximum(c, d_), jnp.minimum(c, d_)
        return jnp.maximum(hi1, hi2) + jnp.maximum(jnp.minimum(hi1, hi2), jnp.maximum(lo1, lo2))

    best = top2_sum(*sel[0:4])
    gidx = jnp.zeros((1, tm), jnp.int32)
    for g in range(1, N_GROUPS):
        sc = top2_sum(*sel[4 * g:4 * g + 4])
        better = sc > best
        gidx = jnp.where(better, g, gidx)
        best = jnp.maximum(best, sc)

    def pick(vals, j):
        out = vals[j]
        for g in range(1, N_GROUPS):
            out = jnp.where(gidx == g, vals[4 * g + j], out)
        return out

    sg = [pick(sel, j) for j in range(EXPERTS_PER_GROUP)]
    pg = [pick(probs, j) for j in range(EXPERTS_PER_GROUP)]
    picked = []
    for j in range(EXPERTS_PER_GROUP):
        rank = jnp.zeros((1, tm), jnp.int32)
        for o in range(EXPERTS_PER_GROUP):
            if o == j:
                continue
            ahead = (sg[o] > sg[j]) if o > j else (sg[o] >= sg[j])
            rank = rank + ahead.astype(jnp.int32)
        picked.append(rank < TOP_K)
    chosen = [jnp.where(picked[j], pg[j], 0.0) for j in range(EXPERTS_PER_GROUP)]
    wsum = functools.reduce(jnp.add, chosen)
    wn = [cj / wsum for cj in chosen]

    pair = jnp.where(picked[0],
                     jnp.where(picked[1], 0, jnp.where(picked[2], 1, 2)),
                     jnp.where(picked[1], jnp.where(picked[2], 3, 4), 5))
    w_lo = jnp.where(picked[0], wn[0], jnp.where(picked[1], wn[1], wn[2]))
    w_hi = jnp.where(picked[3], wn[3], jnp.where(picked[2], wn[2], wn[1]))
    bucket = gidx * len(PAIRS) + pair

    onehot = jnp.where(lax.broadcasted_iota(jnp.int32, (BUCKET_ROWS, tm), 0) == bucket, 1.0, 0.0)
    before = jnp.dot(onehot.astype(BF16), upper_scr[...], preferred_element_type=F32)
    seen = before + cnt_scr[:, 0:1]
    bucket_ref[...] = bucket
    rank_ref[...] = jnp.sum(onehot * seen, axis=0, keepdims=True).astype(jnp.int32)
    cnt_scr[...] = cnt_scr[...] + jnp.sum(onehot, axis=1, keepdims=True)
    cnt_ref[...] = cnt_scr[...]

    pay_t = jnp.concatenate([w_lo, w_hi, jnp.zeros((LANES - TOP_K, tm), F32)], axis=0)
    haug_ref[:, 0:d] = x
    haug_ref[:, d:] = pay_t.T


def _mixout_route_kernel(*refs):
    bias_ref, *y_refs, w_ref, h_ref, g_ref, b_ref, rw_ref = refs[:-6]
    acc = ALPHA * h_ref[...]
    off = 0
    for y_ref in y_refs:
        kd = y_ref.shape[1]
        acc = acc + jnp.dot(y_ref[...], w_ref[off:off + kd, :], preferred_element_type=F32)
        off += kd
    _route_tokens(_layer_norm(acc, g_ref[...], b_ref[...]), bias_ref, rw_ref, *refs[-6:])


def _mixout_route(ys, w, h2d, g, b, router_w_t, router_bias, name):
    n, d = h2d.shape
    tm = _pick(n, (640, 512, 256, 128))
    rows = lambda width: pl.BlockSpec((tm, width), lambda i: (i, 0))
    whole = lambda a: pl.BlockSpec(a.shape, lambda i: (0, 0))
    per_tile = pl.BlockSpec((None, 1, tm), lambda i: (i, 0, 0))
    return pl.pallas_call(
        _mixout_route_kernel,
        grid=(n // tm,),
        in_specs=[pl.BlockSpec(memory_space=pltpu.SMEM)] + [rows(y.shape[1]) for y in ys]
        + [whole(w), rows(d), whole(g), whole(b), whole(router_w_t)],
        out_specs=[rows(d + LANES), per_tile, per_tile, pl.BlockSpec((BUCKET_ROWS, LANES), lambda i: (0, 0))],
        out_shape=[jax.ShapeDtypeStruct((n, d + LANES), F32),
                   jax.ShapeDtypeStruct((n // tm, 1, tm), jnp.int32),
                   jax.ShapeDtypeStruct((n // tm, 1, tm), jnp.int32),
                   jax.ShapeDtypeStruct((BUCKET_ROWS, LANES), F32)],
        scratch_shapes=[pltpu.VMEM((tm, tm), BF16), pltpu.VMEM((BUCKET_ROWS, LANES), F32)],
        compiler_params=_cparams(("arbitrary",)),
        name=name,
    )(router_bias, *ys, w, h2d, g, b, router_w_t)


def _row_dma(src_ref, dst_ref, sem, src_row, dst_row):
    return pltpu.make_async_copy(src_ref.at[pl.ds(src_row, 1), :], dst_ref.at[pl.ds(dst_row, 1), :], sem)


def _scatter_rows_kernel(idx_ref, src_ref, dst_ref, sem, *, rows):
    base = pl.program_id(0) * rows

    def issue(r, c):
        _row_dma(src_ref, dst_ref, sem, r, idx_ref[base + r]).start()
        return c

    def drain(r, c):
        _row_dma(src_ref, dst_ref, sem, 0, 0).wait()
        return c

    lax.fori_loop(0, rows, issue, 0, unroll=8)
    lax.fori_loop(0, rows, drain, 0, unroll=8)


def _gather_rows_kernel(idx_ref, src_ref, dst_ref, sem, *, rows):
    base = pl.program_id(0) * rows

    def issue(r, c):
        _row_dma(src_ref, dst_ref, sem, idx_ref[base + r], r).start()
        return c

    def drain(r, c):
        _row_dma(src_ref, dst_ref, sem, 0, 0).wait()
        return c

    lax.fori_loop(0, rows, issue, 0, unroll=8)
    lax.fori_loop(0, rows, drain, 0, unroll=8)


def _permute_rows(body, idx, src, n_dst, name):
    m = idx.shape[0]
    width = src.shape[1]
    rows = _pick(m, (512, 256, 128))
    tile = pl.BlockSpec((rows, width), lambda i, idx_ref: (i, 0))
    whole = pl.BlockSpec(memory_space=pl.ANY)
    scatter = body is _scatter_rows_kernel
    return pl.pallas_call(
        functools.partial(body, rows=rows),
        grid_spec=pltpu.PrefetchScalarGridSpec(
            num_scalar_prefetch=1,
            grid=(m // rows,),
            in_specs=[tile if scatter else whole],
            out_specs=whole if scatter else tile,
            scratch_shapes=[pltpu.SemaphoreType.DMA(())]),
        out_shape=jax.ShapeDtypeStruct((n_dst, width), src.dtype),
        compiler_params=_cparams(("arbitrary",)),
        name=name,
    )(idx, src)


def _moe_kernel(blk_ref, e_lo_ref, e_hi_ref, nv_ref, x_ref, wg_lo, wg_hi, wu_lo, wu_hi, wd_lo, wd_hi,
                g_ref, b_ref, o_ref):
    nv = nv_ref[pl.program_id(0)]

    @pl.when(nv > 0)
    def _():
        tm, d = o_ref.shape
        live = lax.broadcasted_iota(jnp.int32, (tm, 1), 0) < nv
        x = jnp.where(live, x_ref[:, 0:d], 0.0)
        cw = jnp.where(live, x_ref[:, d:], 0.0)
        xb = x.astype(BF16)
        acc = ALPHA * x
        for j, (wg, wu, wd) in enumerate(((wg_lo, wu_lo, wd_lo), (wg_hi, wu_hi, wd_hi))):
            gate = jnp.dot(xb, wg[...], preferred_element_type=F32)
            up = jnp.dot(xb, wu[...], preferred_element_type=F32)
            act = gate * jax.nn.sigmoid(gate) * up * cw[:, j:j + 1]
            acc = acc + jnp.dot(act.astype(BF16), wd[...], preferred_element_type=F32)
        o_ref[...] = _layer_norm(acc, g_ref[...], b_ref[...])


def _moe(x_sorted, tables, wg, wu, wd, g, b, tm, name):
    rows, wa = x_sorted.shape
    d = wa - LANES
    n_tiles = tables[0].shape[0]
    rows_of = lambda t, blk, e_lo, e_hi, nv: (blk[t], 0)
    lo = lambda t, blk, e_lo, e_hi, nv: (e_lo[t], 0, 0)
    hi = lambda t, blk, e_lo, e_hi, nv: (e_hi[t], 0, 0)
    const = lambda t, blk, e_lo, e_hi, nv: (0, 0)
    w_in = lambda f: pl.BlockSpec((None, d, D_EXPERT), f)
    w_out = lambda f: pl.BlockSpec((None, D_EXPERT, d), f)
    return pl.pallas_call(
        _moe_kernel,
        grid_spec=pltpu.PrefetchScalarGridSpec(
            num_scalar_prefetch=4,
            grid=(n_tiles,),
            in_specs=[pl.BlockSpec((tm, wa), rows_of),
                      w_in(lo), w_in(hi), w_in(lo), w_in(hi), w_out(lo), w_out(hi),
                      pl.BlockSpec((1, d), const), pl.BlockSpec((1, d), const)],
            out_specs=pl.BlockSpec((tm, d), rows_of)),
        out_shape=jax.ShapeDtypeStruct((rows, d), F32),
        compiler_params=_cparams(("arbitrary",)),
        name=name,
    )(*tables, x_sorted, wg, wg, wu, wu, wd, wd, g, b)


def _bucket_layout(counts, tm, n_tiles):
    tiles_b = (counts + tm - 1) // tm
    ends = jnp.cumsum(tiles_b)
    starts = ends - tiles_b
    t = jnp.arange(n_tiles, dtype=jnp.int32)
    tc = jnp.minimum(t, ends[-1] - 1)
    bkt = jnp.sum((tc[:, None] >= ends[None, :]).astype(jnp.int32), axis=1)
    nv = jnp.where(t < ends[-1], jnp.clip(counts[bkt] - (tc - starts[bkt]) * tm, 0, tm), 0)
    pair_lo = jnp.array([p[0] for p in PAIRS], jnp.int32)
    pair_hi = jnp.array([p[1] for p in PAIRS], jnp.int32)
    grp, pair = bkt // len(PAIRS), bkt % len(PAIRS)
    e_lo = grp * EXPERTS_PER_GROUP + pair_lo[pair]
    e_hi = grp * EXPERTS_PER_GROUP + pair_hi[pair]
    i32 = lambda a: a.astype(jnp.int32)
    return i32(starts * tm), (i32(tc), i32(e_lo), i32(e_hi), i32(nv))


def _routed_moe(routed, wg, wu, wd, g, b, out_idx_fn, n_out, name):
    haug, bucket, rank, cnt = routed
    n = haug.shape[0]
    tm = _pick(n, (256, 128))
    n_tiles = n // tm + N_BUCKETS
    counts = cnt[:N_BUCKETS, 0].astype(jnp.int32)
    row0, tables = _bucket_layout(counts, tm, n_tiles)
    pos = row0[bucket.reshape(n)] + rank.reshape(n)
    x_sorted = _permute_rows(_scatter_rows_kernel, pos, haug, n_tiles * tm, name + "_scatter")
    y_sorted = _moe(x_sorted, tables, wg, wu, wd, g, b, tm, name)
    return _permute_rows(_gather_rows_kernel, out_idx_fn(pos), y_sorted, n_out, name + "_gather")


def kernel(x, meta, even_w_in, pool_w, pool_scale, diff_lq1, diff_lk1, diff_lq2, diff_lk2, diff_subln,
           even_w_out, odd_w_in, gla_gate_w2, gla_gate_b, gla_head_norm, odd_w_out, ln_mix_g, ln_mix_b,
           ln_ffn_g, ln_ffn_b, router_w, router_bias, moe_w_gate, moe_w_up, moe_w_down):
    bsz, seq, d = x.shape
    l = PREFIX + seq
    n = bsz * l
    pad = jnp.zeros((bsz, FIRST_VALID, d), x.dtype)
    metas = jnp.broadcast_to(meta.astype(x.dtype)[None], (bsz, N_META, d))
    h = jnp.concatenate([pad, metas, x], axis=1).reshape(n, d)
    router_w_t = router_w.T
    row = lambda a: a.reshape(1, -1)

    for i in range(DEPTH):
        j = i // 2
        if i % 2 == 0:
            lambda_init = 0.8 - 0.6 * math.exp(-0.3 * i)
            z = _proj(h, even_w_in[j].astype(BF16), "even_in_proj").reshape(bsz, l, EVEN_IN)
            y_pool = _pool(z, pool_w[j].astype(BF16), row(pool_scale[j]))
            lam_params = jnp.stack([diff_lq1[j], diff_lk1[j], diff_lq2[j], diff_lk2[j]])
            y_attn = _attn(z, lam_params, diff_subln[j].reshape(-1, 1), lambda_init)
            ys = [y_pool.reshape(n, POOL_WIDTH), y_attn.reshape(n, DIFF_WIDTH)]
            routed = _mixout_route(ys, even_w_out[j].astype(BF16), h, row(ln_mix_g[i]), row(ln_mix_b[i]),
                                   router_w_t, router_bias, "even_out_proj")
        else:
            w_in = jnp.pad(odd_w_in[j], ((0, 0), (0, ODD_IN_PAD - ODD_IN))).astype(BF16)
            z = _proj(h, w_in, "odd_in_proj").reshape(bsz, l, ODD_IN_PAD)
            w2 = jnp.pad(gla_gate_w2[j], ((0, LANES - GLA_RANK), (0, 0))).astype(BF16)
            y = _gla(z, w2, row(gla_gate_b[j]), row(gla_head_norm[j]))
            routed = _mixout_route([y.reshape(n, GLA_V_WIDTH)], odd_w_out[j].astype(BF16), h,
                                   row(ln_mix_g[i]), row(ln_mix_b[i]), router_w_t, router_bias, "odd_out_proj")
        last = i == DEPTH - 1
        if last:
            out_idx_fn = lambda pos: pos.reshape(bsz, l)[:, PREFIX:].reshape(-1)
        else:
            out_idx_fn = lambda pos: pos
        h = _routed_moe(routed, moe_w_gate[i].astype(BF16), moe_w_up[i].astype(BF16),
                        moe_w_down[i].astype(BF16), row(ln_ffn_g[i]), row(ln_ffn_b[i]),
                        out_idx_fn, bsz * seq if last else n, f"moe_{i}")
    return h.reshape(bsz, seq, d)
```

```python
import functools
import math

import jax
import jax.numpy as jnp
from jax import lax
from jax.experimental import pallas as pl
from jax.experimental.pallas import tpu as pltpu

F32 = jnp.float32
BF16 = jnp.bfloat16

D_MODEL = 1024
DEPTH = 2
N_META = 16
PREFIX = 128
FIRST_VALID = PREFIX - N_META
POOL_WINDOWS = (2, 4, 8, 16)
POOL_GROUP = 128
POOL_WIDTH = 512
MAX_WINDOW = 16
DIFF_HEADS = 4
DIFF_QK_DIM = 64
DIFF_V_DIM = 128
DIFF_WIDTH = 512
EVEN_IN = 2048
GLA_HEADS = 4
GLA_DK = 128
GLA_DV = 256
GLA_RANK = 16
GLA_TAU = 16.0
GLA_QK_WIDTH = GLA_HEADS * GLA_DK
GLA_V_WIDTH = GLA_HEADS * GLA_DV
ODD_IN = 2 * GLA_QK_WIDTH + 2 * GLA_V_WIDTH + GLA_RANK
ODD_IN_PAD = 3200
N_EXPERTS = 16
N_GROUPS = 4
EXPERTS_PER_GROUP = 4
D_EXPERT = 512
TOP_K = 2
PAIRS = ((0, 1), (0, 2), (0, 3), (1, 2), (1, 3), (2, 3))
N_BUCKETS = N_GROUPS * len(PAIRS)
BUCKET_ROWS = 32
ALPHA = (2.0 * DEPTH) ** 0.25
NEG_BIG = -1e30
KT = 512
ONES_ROWS = 16

LANES = 128
VMEM_LIMIT = 56 * 1024 * 1024


def _pick(n, candidates):
    for c in candidates:
        if n % c == 0:
            return c
    raise ValueError(f"no tile for {n} in {candidates}")


def _cparams(sem):
    return pltpu.CompilerParams(dimension_semantics=sem, vmem_limit_bytes=VMEM_LIMIT)


def _layer_norm(a, g, b):
    mu = jnp.mean(a, axis=-1, keepdims=True)
    d = a - mu
    var = jnp.mean(d * d, axis=-1, keepdims=True)
    return d * lax.rsqrt(var + 1e-5) * g + b


def _proj_kernel(x_ref, w_ref, o_ref, *, tn):
    xb = x_ref[...].astype(BF16)
    for j in range(o_ref.shape[1] // tn):
        cols = slice(j * tn, (j + 1) * tn)
        o_ref[:, cols] = jnp.dot(xb, w_ref[:, cols], preferred_element_type=F32).astype(o_ref.dtype)


def _proj(x2d, w, name):
    n, k = x2d.shape
    wout = w.shape[1]
    tm = _pick(n, (640, 512, 256, 128))
    tn = _pick(wout, (640, 512, 128))
    return pl.pallas_call(
        functools.partial(_proj_kernel, tn=tn),
        grid=(n // tm,),
        in_specs=[pl.BlockSpec((tm, k), lambda i: (i, 0)),
                  pl.BlockSpec((k, wout), lambda i: (0, 0))],
        out_specs=pl.BlockSpec((tm, wout), lambda i: (i, 0)),
        out_shape=jax.ShapeDtypeStruct((n, wout), BF16),
        compiler_params=_cparams(("parallel",)),
        name=name,
    )(x2d, w)


def _pool_kernel(cur_ref, halo_ref, w_ref, sc_ref, o_ref, u_scr):
    t = pl.program_id(1)
    tl = cur_ref.shape[0]
    pos = t * tl + lax.broadcasted_iota(jnp.int32, (tl, 1), 0)
    hpos = t * tl - MAX_WINDOW + lax.broadcasted_iota(jnp.int32, (MAX_WINDOW, 1), 0)
    u_scr[0:MAX_WINDOW, :] = jnp.where(hpos >= FIRST_VALID, halo_ref[...].astype(F32), 0.0)
    u_scr[MAX_WINDOW:, :] = jnp.where(pos >= FIRST_VALID, cur_ref[...].astype(F32), 0.0)
    n_valid = pos - (FIRST_VALID - 1)
    for gi, w in enumerate(POOL_WINDOWS):
        cols = slice(gi * POOL_GROUP, (gi + 1) * POOL_GROUP)
        u = u_scr[MAX_WINDOW:MAX_WINDOW + tl, cols]
        win = u
        for j in range(1, w):
            win = win + u_scr[MAX_WINDOW - j:MAX_WINDOW - j + tl, cols]
        cnt = jnp.clip(n_valid, 1, w).astype(F32)
        d = win / cnt - u
        y = jnp.dot(d.astype(BF16), w_ref[gi], preferred_element_type=F32) * sc_ref[:, cols]
        o_ref[:, cols] = y.astype(o_ref.dtype)


def _pool(z, pool_w, pool_scale):
    b, l, _ = z.shape
    tl = _pick(l, (640, 128))
    hb = tl // MAX_WINDOW
    return pl.pallas_call(
        _pool_kernel,
        grid=(b, l // tl),
        in_specs=[pl.BlockSpec((None, tl, POOL_WIDTH), lambda bi, t: (bi, t, 0)),
                  pl.BlockSpec((None, MAX_WINDOW, POOL_WIDTH),
                               lambda bi, t: (bi, jnp.maximum(t * hb - 1, 0), 0)),
                  pl.BlockSpec((len(POOL_WINDOWS), POOL_GROUP, POOL_GROUP), lambda bi, t: (0, 0, 0)),
                  pl.BlockSpec((1, POOL_WIDTH), lambda bi, t: (0, 0))],
        out_specs=pl.BlockSpec((None, tl, POOL_WIDTH), lambda bi, t: (bi, t, 0)),
        out_shape=jax.ShapeDtypeStruct((b, l, POOL_WIDTH), BF16),
        scratch_shapes=[pltpu.VMEM((tl + MAX_WINDOW, POOL_WIDTH), F32)],
        compiler_params=_cparams(("parallel", "parallel")),
        name="pool_mixer",
    )(z, z, pool_w, pool_scale)


def _attn_kernel(lam_ref, q_ref, k_ref, v_ref, sub_ref, o_ref, ka_scr, kb_scr, vt_scr,
                 s0_scr, s1_scr, t0_scr, t1_scr, m0_scr, m1_scr, l0_scr, l1_scr, acc0_scr, acc1_scr,
                 *, tq, lambda_init):
    h = pl.program_id(1)
    i = pl.program_id(2)
    slope = jnp.exp2(-2.0 * (h + 1).astype(F32))
    half = DIFF_QK_DIM
    n_chunks = k_ref.shape[0] // LANES
    n_tail, n_pad = _attn_tail(tq)

    @pl.when(i == 0)
    def _():
        lane = lax.broadcasted_iota(jnp.int32, (LANES, 2 * half), 1)
        rowf = lax.broadcasted_iota(jnp.int32, (LANES, 2 * half), 0).astype(F32)

        def build(c, carry):
            rows = pl.ds(pl.multiple_of(c * LANES, LANES), LANES)
            kt = k_ref[rows, :].astype(F32)
            hi = jnp.full((LANES, 2 * half), c, jnp.int32).astype(F32)
            ka = jnp.where(lane < half, kt, jnp.where(lane == half, hi, jnp.where(lane == half + 1, rowf, 0.0)))
            kb = jnp.where(lane >= half, kt, jnp.where(lane == 0, hi, jnp.where(lane == 1, rowf, 0.0)))
            ka_scr[rows, :] = ka.astype(BF16)
            kb_scr[rows, :] = kb.astype(BF16)
            vt_scr[c, 0:DIFF_V_DIM, :] = v_ref[rows, :].astype(F32).T.astype(BF16)
            vt_scr[c, DIFF_V_DIM:, :] = jnp.ones((ONES_ROWS, LANES), BF16)
            return carry

        lax.fori_loop(0, n_chunks, build, 0)
        for extra in range(n_pad):
            pad_rows = pl.ds((n_chunks + extra) * LANES, LANES)
            ka_scr[pad_rows, :] = jnp.zeros((LANES, 2 * half), BF16)
            kb_scr[pad_rows, :] = jnp.zeros((LANES, 2 * half), BF16)
            vt_scr[n_chunks + extra] = jnp.zeros((DIFF_V_DIM + ONES_ROWS, LANES), BF16)

    lane = lax.broadcasted_iota(jnp.int32, (tq, 2 * half), 1)
    q = (q_ref[...] * jnp.asarray(half ** -0.5, BF16)).astype(F32)
    f_hi = LANES * slope
    qa = jnp.where(lane < half, q, jnp.where(lane == half, f_hi, jnp.where(lane == half + 1, slope, 0.0)))
    qb = jnp.where(lane >= half, q, jnp.where(lane == 0, f_hi, jnp.where(lane == 1, slope, 0.0)))
    q_maps = (qa.T.astype(BF16), qb.T.astype(BF16))
    k_maps = (ka_scr, kb_scr)
    s_scrs, t_scrs = (s0_scr, s1_scr), (t0_scr, t1_scr)
    m_scrs, l_scrs, acc_scrs = (m0_scr, m1_scr), (l0_scr, l1_scr), (acc0_scr, acc1_scr)
    for mp in range(2):
        m_scrs[mp][...] = jnp.full(m0_scr.shape, NEG_BIG, F32)
        l_scrs[mp][...] = jnp.zeros(l0_scr.shape, F32)
        acc_scrs[mp][...] = jnp.zeros(acc0_scr.shape, F32)
    q0 = i * tq
    n_full = q0 // KT

    def scores(t, mode):
        k0 = pl.multiple_of(t * KT, KT)
        buf = t % 2
        allowed = None
        if mode is not None:
            kpos = k0 + lax.broadcasted_iota(jnp.int32, (KT, tq), 0)
            allowed = kpos >= FIRST_VALID
            if mode == "diag":
                qpos = q0 + lax.broadcasted_iota(jnp.int32, (KT, tq), 1)
                allowed = jnp.logical_and(kpos <= qpos, jnp.logical_or(allowed, kpos == qpos))
        for mp in range(2):
            s = jnp.dot(k_maps[mp][pl.ds(k0, KT), :], q_maps[mp], preferred_element_type=F32)
            if allowed is not None:
                s = jnp.where(allowed, s, NEG_BIG)
            s_scrs[mp][buf] = s
            t_scrs[mp][buf] = jnp.max(s, axis=0, keepdims=True)

    def accumulate(t):
        c0 = t * (KT // LANES)
        buf = t % 2
        vt = jnp.concatenate([vt_scr[c0 + u] for u in range(KT // LANES)], axis=1)
        for mp in range(2):
            m_prev = m_scrs[mp][...]
            m_new = jnp.maximum(m_prev, t_scrs[mp][buf])
            alpha = jnp.exp(m_prev - m_new)
            p = jnp.exp(s_scrs[mp][buf] - m_new)
            pv = jnp.dot(vt, p.astype(BF16), preferred_element_type=F32)
            l_scrs[mp][...] = alpha * l_scrs[mp][...] + pv[DIFF_V_DIM:DIFF_V_DIM + 1, :]
            acc_scrs[mp][...] = alpha * acc_scrs[mp][...] + pv[0:DIFF_V_DIM, :]
            m_scrs[mp][...] = m_new

    @pl.when(n_full >= 1)
    def _():
        scores(0, "valid")

    @pl.when(n_full == 0)
    def _():
        scores(0, "diag")

    def steady(t, c):
        accumulate(t)
        scores(t + 1, None)
        return c

    lax.fori_loop(0, n_full - 1, steady, 0)

    @pl.when(n_full >= 1)
    def _():
        accumulate(n_full - 1)
        scores(n_full, "diag")

    for u in range(n_tail - 1):
        accumulate(n_full + u)
        scores(n_full + u + 1, "diag")
    accumulate(n_full + n_tail - 1)

    lam = (jnp.exp(jnp.sum(lam_ref[0:1, :] * lam_ref[1:2, :], axis=-1, keepdims=True))
           - jnp.exp(jnp.sum(lam_ref[2:3, :] * lam_ref[3:4, :], axis=-1, keepdims=True)) + lambda_init)
    o = acc0_scr[...] / l0_scr[...] - lam * (acc1_scr[...] / l1_scr[...])
    o = o * lax.rsqrt(jnp.mean(o * o, axis=0, keepdims=True) + 1e-6) * sub_ref[...] * (1.0 - lambda_init)
    o_ref[...] = o.T.astype(o_ref.dtype)


def _attn_tail(tq):
    n_tail = -(-(tq + KT - LANES) // KT)
    return n_tail, (n_tail * KT - tq) // LANES


def _attn(z, lam_params, subln_col, lambda_init):
    b, l, _ = z.shape
    tq = 5 * LANES
    assert l % tq == 0 and tq % LANES == 0 and KT % LANES == 0
    n_pad = _attn_tail(tq)[1]
    qb, kb, vb = POOL_WIDTH // LANES, (POOL_WIDTH + DIFF_WIDTH) // LANES, (POOL_WIDTH + 2 * DIFF_WIDTH) // LANES
    row = pltpu.VMEM((1, tq), F32)
    acc = pltpu.VMEM((DIFF_V_DIM, tq), F32)
    sbuf = pltpu.VMEM((2, KT, tq), F32)
    tbuf = pltpu.VMEM((2, 1, tq), F32)
    return pl.pallas_call(
        functools.partial(_attn_kernel, tq=tq, lambda_init=lambda_init),
        grid=(b, DIFF_HEADS, l // tq),
        in_specs=[pl.BlockSpec((4, DIFF_QK_DIM), lambda bi, h, i: (0, 0)),
                  pl.BlockSpec((None, tq, LANES), lambda bi, h, i: (bi, i, qb + h)),
                  pl.BlockSpec((None, l, LANES), lambda bi, h, i: (bi, 0, kb + h)),
                  pl.BlockSpec((None, l, LANES), lambda bi, h, i: (bi, 0, vb + h)),
                  pl.BlockSpec((DIFF_V_DIM, 1), lambda bi, h, i: (0, 0))],
        out_specs=pl.BlockSpec((None, tq, DIFF_V_DIM), lambda bi, h, i: (bi, i, h)),
        out_shape=jax.ShapeDtypeStruct((b, l, DIFF_WIDTH), BF16),
        scratch_shapes=[pltpu.VMEM((l + n_pad * LANES, LANES), BF16), pltpu.VMEM((l + n_pad * LANES, LANES), BF16),
                        pltpu.VMEM((l // LANES + n_pad, DIFF_V_DIM + ONES_ROWS, LANES), BF16),
                        sbuf, sbuf, tbuf, tbuf, row, row, row, row, acc, acc],
        compiler_params=_cparams(("parallel", "parallel", "arbitrary")),
        name="diff_attn",
    )(lam_params, z, z, z, subln_col)


def _gla_kernel(q_ref, k_ref, v_ref, r_ref, glr_ref, w2_ref, gb_ref, hn_ref, o_ref, st_ref, *, c_len, sb):
    c = pl.program_id(0)
    n_batch = q_ref.shape[0]

    @pl.when(c == 0)
    def _():
        st_ref[...] = jnp.zeros(st_ref.shape, F32)

    pos = c * c_len + lax.broadcasted_iota(jnp.int32, (c_len, 1), 0)
    validf = (pos >= FIRST_VALID).astype(F32)
    ri = lax.broadcasted_iota(jnp.int32, (c_len, c_len), 0)
    ci = lax.broadcasted_iota(jnp.int32, (c_len, c_len), 1)
    tri = jnp.where(ri >= ci, 1.0, 0.0).astype(BF16)
    row = lax.broadcasted_iota(jnp.int32, (c_len, 1), 0)
    t_loc = lax.broadcasted_iota(jnp.int32, (sb, 1), 0)
    lane_c = lax.broadcasted_iota(jnp.int32, (sb, c_len), 1)
    nt = (((1,), (1,)), ((), ()))
    tn = (((0,), (0,)), ((), ()))

    for bi, h in [(bi, h) for bi in range(n_batch) for h in range(GLA_HEADS)]:
        glr = glr_ref[bi]
        ks = slice(h * GLA_DK, (h + 1) * GLA_DK)
        vs = slice(h * GLA_DV, (h + 1) * GLA_DV)
        g = jnp.dot(glr, w2_ref[:, ks], preferred_element_type=F32) + gb_ref[:, ks]
        log_a = -(jnp.maximum(-g, 0.0) + jnp.log(1.0 + jnp.exp(-jnp.abs(g)))) * (1.0 / GLA_TAU)
        la_hi = log_a.astype(BF16)
        la_lo = (log_a - la_hi.astype(F32)).astype(BF16)
        b = (jnp.dot(tri, la_hi, preferred_element_type=F32)
             + jnp.dot(tri, la_lo, preferred_element_type=F32))
        q = q_ref[bi, :, ks].astype(F32) * (GLA_DK ** -0.5)
        k = k_ref[bi, :, ks].astype(F32) * validf
        v = v_ref[bi, :, vs]
        st = st_ref[bi, h]

        blocks = []
        for i in range(c_len // sb):
            lo = i * sb
            q_i = q[lo:lo + sb]
            b_i = b[lo:lo + sb]
            a_i = jnp.zeros((sb, c_len), F32)
            if i > 0:
                b_ref_row = b[lo - 1:lo]
                q_t = q_i * jnp.exp(b_i - b_ref_row)
                k_t = jnp.where(row < lo, k * jnp.exp(jnp.minimum(b_ref_row - b, 0.0)), 0.0)
                a_i = lax.dot_general(q_t.astype(BF16), k_t.astype(BF16), nt, preferred_element_type=F32)
            for s in range(sb):
                r = lo + s
                e = jnp.exp(b_i - b[r:r + 1])
                col = jnp.sum(q_i * k[r:r + 1] * e, axis=-1, keepdims=True)
                col = jnp.where(t_loc >= s, col, 0.0)
                a_i = jnp.where(lane_c == r, col, a_i)
            blocks.append(a_i)
        att = jnp.concatenate(blocks, axis=0)

        o = jnp.dot(att.astype(BF16), v, preferred_element_type=F32)
        o = o + lax.dot_general((q * jnp.exp(b)).astype(BF16), st.astype(BF16), nt,
                                preferred_element_type=F32)
        b_last = b[c_len - 1:c_len]
        k_hat = (k * jnp.exp(b_last - b)).astype(BF16)
        st_ref[bi, h] = st * jnp.exp(b_last) + lax.dot_general(v, k_hat, tn, preferred_element_type=F32)

        o = o * lax.rsqrt(jnp.mean(o * o, axis=-1, keepdims=True) + 1e-6) * hn_ref[...]
        rg = r_ref[bi, :, vs].astype(F32)
        o_ref[bi, :, vs] = (o * (rg * jax.nn.sigmoid(rg))).astype(o_ref.dtype)


def _gla(z, w2, gate_b, head_norm):
    b, l, _ = z.shape
    c_len = 128
    qw, vw = GLA_QK_WIDTH, GLA_V_WIDTH
    return pl.pallas_call(
        functools.partial(_gla_kernel, c_len=c_len, sb=16),
        grid=(l // c_len,),
        in_specs=[pl.BlockSpec((b, c_len, qw), lambda c: (0, c, 0)),
                  pl.BlockSpec((b, c_len, qw), lambda c: (0, c, 1)),
                  pl.BlockSpec((b, c_len, vw), lambda c: (0, c, 1)),
                  pl.BlockSpec((b, c_len, vw), lambda c: (0, c, 2)),
                  pl.BlockSpec((b, c_len, LANES), lambda c: (0, c, (2 * qw + 2 * vw) // LANES)),
                  pl.BlockSpec((LANES, qw), lambda c: (0, 0)),
                  pl.BlockSpec((1, qw), lambda c: (0, 0)),
                  pl.BlockSpec((1, GLA_DV), lambda c: (0, 0))],
        out_specs=pl.BlockSpec((b, c_len, vw), lambda c: (0, c, 0)),
        out_shape=jax.ShapeDtypeStruct((b, l, vw), BF16),
        scratch_shapes=[pltpu.VMEM((b, GLA_HEADS, GLA_DV, GLA_DK), F32)],
        compiler_params=_cparams(("arbitrary",)),
        name="gla_mixer",
    )(z, z, z, z, z, w2, gate_b, head_norm)


def _route_tokens(x, bias_ref, rw_ref, haug_ref, bucket_ref, rank_ref, cnt_ref, upper_scr, cnt_scr):
    tm, d = x.shape

    @pl.when(pl.program_id(0) == 0)
    def _():
        r = lax.broadcasted_iota(jnp.int32, (tm, tm), 0)
        c = lax.broadcasted_iota(jnp.int32, (tm, tm), 1)
        upper_scr[...] = jnp.where(r < c, 1.0, 0.0).astype(BF16)
        cnt_scr[...] = jnp.zeros(cnt_scr.shape, F32)

    w = rw_ref[...]
    xh = x.astype(BF16)
    xl = (x - xh.astype(F32)).astype(BF16)
    wh = w.astype(BF16)
    wl = (w - wh.astype(F32)).astype(BF16)
    nt = (((1,), (1,)), ((), ()))
    logits = (lax.dot_general(wh, xh, nt, preferred_element_type=F32)
              + lax.dot_general(wh, xl, nt, preferred_element_type=F32)
              + lax.dot_general(wl, xh, nt, preferred_element_type=F32))
    lg = [logits[e:e + 1, :] for e in range(N_EXPERTS)]
    mx = functools.reduce(jnp.maximum, lg)
    ex = [jnp.exp(v - mx) for v in lg]
    den = functools.reduce(jnp.add, ex)
    probs = [v / den for v in ex]
    sel = [probs[e] + bias_ref[e] for e in range(N_EXPERTS)]

    def top2_sum(a, b, c, d_):
        hi1, lo1 = jnp.maximum(a, b), jnp.minimum(a, b)
        hi2, lo2 = jnp.maximum(c, d_), jnp.minimum(c, d_)
        return jnp.maximum(hi1, hi2) + jnp.maximum(jnp.minimum(hi1, hi2), jnp.maximum(lo1, lo2))

    best = top2_sum(*sel[0:4])
    gidx = jnp.zeros((1, tm), jnp.int32)
    for g in range(1, N_GROUPS):
        sc = top2_sum(*sel[4 * g:4 * g + 4])
        better = sc > best
        gidx = jnp.where(better, g, gidx)
        best = jnp.maximum(best, sc)

    def pick(vals, j):
        out = vals[j]
        for g in range(1, N_GROUPS):
            out = jnp.where(gidx == g, vals[4 * g + j], out)
        return out

    sg = [pick(sel, j) for j in range(EXPERTS_PER_GROUP)]
    pg = [pick(probs, j) for j in range(EXPERTS_PER_GROUP)]
    picked = []
    for j in range(EXPERTS_PER_GROUP):
        rank = jnp.zeros((1, tm), jnp.int32)
        for o in range(EXPERTS_PER_GROUP):
            if o == j:
                continue
            ahead = (sg[o] > sg[j]) if o > j else (sg[o] >= sg[j])
            rank = rank + ahead.astype(jnp.int32)
        picked.append(rank < TOP_K)
    chosen = [jnp.where(picked[j], pg[j], 0.0) for j in range(EXPERTS_PER_GROUP)]
    wsum = functools.reduce(jnp.add, chosen)
    wn = [cj / wsum for cj in chosen]

    pair = jnp.where(picked[0],
                     jnp.where(picked[1], 0, jnp.where(picked[2], 1, 2)),
                     jnp.where(picked[1], jnp.where(picked[2], 3, 4), 5))
    w_lo = jnp.where(picked[0], wn[0], jnp.where(picked[1], wn[1], wn[2]))
    w_hi = jnp.where(picked[3], wn[3], jnp.where(picked[2], wn[2], wn[1]))
    bucket = gidx * len(PAIRS) + pair

    onehot = jnp.where(lax.broadcasted_iota(jnp.int32, (BUCKET_ROWS, tm), 0) == bucket, 1.0, 0.0)
    before = jnp.dot(onehot.astype(BF16), upper_scr[...], preferred_element_type=F32)
    seen = before + cnt_scr[:, 0:1]
    bucket_ref[...] = bucket
    rank_ref[...] = jnp.sum(onehot * seen, axis=0, keepdims=True).astype(jnp.int32)
    cnt_scr[...] = cnt_scr[...] + jnp.sum(onehot, axis=1, keepdims=True)
    cnt_ref[...] = cnt_scr[...]

    pay_t = jnp.concatenate([w_lo, w_hi, jnp.zeros((LANES - TOP_K, tm), F32)], axis=0)
    haug_ref[:, 0:d] = x
    haug_ref[:, d:] = pay_t.T


def _mixout_route_kernel(*refs):
    bias_ref, *y_refs, w_ref, h_ref, g_ref, b_ref, rw_ref = refs[:-6]
    acc = ALPHA * h_ref[...]
    off = 0
    for y_ref in y_refs:
        kd = y_ref.shape[1]
        acc = acc + jnp.dot(y_ref[...], w_ref[off:off + kd, :], preferred_element_type=F32)
        off += kd
    _route_tokens(_layer_norm(acc, g_ref[...], b_ref[...]), bias_ref, rw_ref, *refs[-6:])


def _mixout_route(ys, w, h2d, g, b, router_w_t, router_bias, name):
    n, d = h2d.shape
    tm = _pick(n, (640, 512, 256, 128))
    rows = lambda width: pl.BlockSpec((tm, width), lambda i: (i, 0))
    whole = lambda a: pl.BlockSpec(a.shape, lambda i: (0, 0))
    per_tile = pl.BlockSpec((None, 1, tm), lambda i: (i, 0, 0))
    return pl.pallas_call(
        _mixout_route_kernel,
        grid=(n // tm,),
        in_specs=[pl.BlockSpec(memory_space=pltpu.SMEM)] + [rows(y.shape[1]) for y in ys]
        + [whole(w), rows(d), whole(g), whole(b), whole(router_w_t)],
        out_specs=[rows(d + LANES), per_tile, per_tile, pl.BlockSpec((BUCKET_ROWS, LANES), lambda i: (0, 0))],
        out_shape=[jax.ShapeDtypeStruct((n, d + LANES), F32),
                   jax.ShapeDtypeStruct((n // tm, 1, tm), jnp.int32),
                   jax.ShapeDtypeStruct((n // tm, 1, tm), jnp.int32),
                   jax.ShapeDtypeStruct((BUCKET_ROWS, LANES), F32)],
        scratch_shapes=[pltpu.VMEM((tm, tm), BF16), pltpu.VMEM((BUCKET_ROWS, LANES), F32)],
        compiler_params=_cparams(("arbitrary",)),
        name=name,
    )(router_bias, *ys, w, h2d, g, b, router_w_t)


def _row_dma(src_ref, dst_ref, sem, src_row, dst_row):
    return pltpu.make_async_copy(src_ref.at[pl.ds(src_row, 1), :], dst_ref.at[pl.ds(dst_row, 1), :], sem)


def _scatter_rows_kernel(idx_ref, src_ref, dst_ref, sem, *, rows):
    base = pl.program_id(0) * rows
    for r in range(rows):
        _row_dma(src_ref, dst_ref, sem, r, idx_ref[base + r]).start(priority=r % 2)

    def drain(r, c):
        _row_dma(src_ref, dst_ref, sem, 0, 0).wait()
        return c

    lax.fori_loop(0, rows, drain, 0, unroll=8)


def _gather_rows_kernel(idx_ref, src_ref, dst_ref, sem, *, rows):
    base = pl.program_id(0) * rows
    for r in range(rows):
        _row_dma(src_ref, dst_ref, sem, idx_ref[base + r], r).start(priority=r % 2)

    def drain(r, c):
        _row_dma(src_ref, dst_ref, sem, 0, 0).wait()
        return c

    lax.fori_loop(0, rows, drain, 0, unroll=8)


def _permute_rows(body, idx, src, n_dst, name):
    m = idx.shape[0]
    width = src.shape[1]
    rows = _pick(m, (512, 256, 128))
    tile = pl.BlockSpec((rows, width), lambda i, idx_ref: (i, 0))
    whole = pl.BlockSpec(memory_space=pl.ANY)
    scatter = body is _scatter_rows_kernel
    return pl.pallas_call(
        functools.partial(body, rows=rows),
        grid_spec=pltpu.PrefetchScalarGridSpec(
            num_scalar_prefetch=1,
            grid=(m // rows,),
            in_specs=[tile if scatter else whole],
            out_specs=whole if scatter else tile,
            scratch_shapes=[pltpu.SemaphoreType.DMA(())]),
        out_shape=jax.ShapeDtypeStruct((n_dst, width), src.dtype),
        compiler_params=_cparams(("arbitrary",)),
        name=name,
    )(idx, src)


def _moe_kernel(blk_ref, e_lo_ref, e_hi_ref, new_lo_ref, new_hi_ref, nv_ref, x_ref,
                wg_lo, wg_hi, wu_lo, wu_hi, wd_lo, wd_hi, g_ref, b_ref, o_ref, wg_scr, wu_scr, wd_scr):
    t = pl.program_id(0)
    nv = nv_ref[t]

    for j, (flag_ref, wg, wu, wd) in enumerate(((new_lo_ref, wg_lo, wu_lo, wd_lo), (new_hi_ref, wg_hi, wu_hi, wd_hi))):
        @pl.when(flag_ref[t] > 0)
        def _():
            wg_scr[j] = wg[...].astype(BF16)
            wu_scr[j] = wu[...].astype(BF16)
            wd_scr[j] = wd[...].astype(BF16)

    @pl.when(nv > 0)
    def _():
        tm, d = o_ref.shape
        live = lax.broadcasted_iota(jnp.int32, (tm, 1), 0) < nv
        x = jnp.where(live, x_ref[:, 0:d], 0.0)
        cw = jnp.where(live, x_ref[:, d:], 0.0)
        xb = x.astype(BF16)
        acc = ALPHA * x
        for j in range(TOP_K):
            gate = jnp.dot(xb, wg_scr[j], preferred_element_type=F32)
            up = jnp.dot(xb, wu_scr[j], preferred_element_type=F32)
            act = gate * jax.nn.sigmoid(gate) * up * cw[:, j:j + 1]
            acc = acc + jnp.dot(act.astype(BF16), wd_scr[j], preferred_element_type=F32)
        o_ref[...] = _layer_norm(acc, g_ref[...], b_ref[...])


def _moe(x_sorted, tables, wg, wu, wd, g, b, tm, name):
    rows, wa = x_sorted.shape
    d = wa - LANES
    n_tiles = tables[0].shape[0]
    rows_of = lambda t, blk, e_lo, e_hi, new_lo, new_hi, nv: (blk[t], 0)
    lo = lambda t, blk, e_lo, e_hi, new_lo, new_hi, nv: (e_lo[t], 0, 0)
    hi = lambda t, blk, e_lo, e_hi, new_lo, new_hi, nv: (e_hi[t], 0, 0)
    const = lambda t, blk, e_lo, e_hi, new_lo, new_hi, nv: (0, 0)
    w_in = lambda f: pl.BlockSpec((None, d, D_EXPERT), f)
    w_out = lambda f: pl.BlockSpec((None, D_EXPERT, d), f)
    return pl.pallas_call(
        _moe_kernel,
        grid_spec=pltpu.PrefetchScalarGridSpec(
            num_scalar_prefetch=6,
            grid=(n_tiles,),
            in_specs=[pl.BlockSpec((tm, wa), rows_of),
                      w_in(lo), w_in(hi), w_in(lo), w_in(hi), w_out(lo), w_out(hi),
                      pl.BlockSpec((1, d), const), pl.BlockSpec((1, d), const)],
            out_specs=pl.BlockSpec((tm, d), rows_of),
            scratch_shapes=[pltpu.VMEM((TOP_K, d, D_EXPERT), BF16), pltpu.VMEM((TOP_K, d, D_EXPERT), BF16),
                            pltpu.VMEM((TOP_K, D_EXPERT, d), BF16)]),
        out_shape=jax.ShapeDtypeStruct((rows, d), F32),
        compiler_params=_cparams(("arbitrary",)),
        name=name,
    )(*tables, x_sorted, wg, wg, wu, wu, wd, wd, g, b)


def _bucket_layout(counts, tm, n_tiles):
    tiles_b = (counts + tm - 1) // tm
    ends = jnp.cumsum(tiles_b)
    starts = ends - tiles_b
    t = jnp.arange(n_tiles, dtype=jnp.int32)
    tc = jnp.minimum(t, ends[-1] - 1)
    bkt = jnp.sum((tc[:, None] >= ends[None, :]).astype(jnp.int32), axis=1)
    nv = jnp.where(t < ends[-1], jnp.clip(counts[bkt] - (tc - starts[bkt]) * tm, 0, tm), 0)
    pair_lo = jnp.array([p[0] for p in PAIRS], jnp.int32)
    pair_hi = jnp.array([p[1] for p in PAIRS], jnp.int32)
    grp, pair = bkt // len(PAIRS), bkt % len(PAIRS)
    e_lo = grp * EXPERTS_PER_GROUP + pair_lo[pair]
    e_hi = grp * EXPERTS_PER_GROUP + pair_hi[pair]
    changed = lambda e: jnp.concatenate([jnp.ones((1,), jnp.int32), (e[1:] != e[:-1]).astype(jnp.int32)])
    i32 = lambda a: a.astype(jnp.int32)
    return i32(starts * tm), (i32(tc), i32(e_lo), i32(e_hi), changed(e_lo), changed(e_hi), i32(nv))


def _routed_moe(routed, wg, wu, wd, g, b, out_idx_fn, n_out, name):
    haug, bucket, rank, cnt = routed
    n = haug.shape[0]
    tm = _pick(n, (256, 128))
    n_tiles = n // tm + N_BUCKETS
    counts = cnt[:N_BUCKETS, 0].astype(jnp.int32)
    row0, tables = _bucket_layout(counts, tm, n_tiles)
    pos = row0[bucket.reshape(n)] + rank.reshape(n)
    x_sorted = _permute_rows(_scatter_rows_kernel, pos, haug, n_tiles * tm, name + "_scatter")
    y_sorted = _moe(x_sorted, tables, wg, wu, wd, g, b, tm, name)
    return _permute_rows(_gather_rows_kernel, out_idx_fn(pos), y_sorted, n_out, name + "_gather")


def kernel(x, meta, even_w_in, pool_w, pool_scale, diff_lq1, diff_lk1, diff_lq2, diff_lk2, diff_subln,
           even_w_out, odd_w_in, gla_gate_w2, gla_gate_b, gla_head_norm, odd_w_out, ln_mix_g, ln_mix_b,
           ln_ffn_g, ln_ffn_b, router_w, router_bias, moe_w_gate, moe_w_up, moe_w_down):
    bsz, seq, d = x.shape
    l = PREFIX + seq
    n = bsz * l
    pad = jnp.zeros((bsz, FIRST_VALID, d), x.dtype)
    metas = jnp.broadcast_to(meta.astype(x.dtype)[None], (bsz, N_META, d))
    h = jnp.concatenate([pad, metas, x], axis=1).reshape(n, d)
    router_w_t = router_w.T
    row = lambda a: a.reshape(1, -1)

    for i in range(DEPTH):
        j = i // 2
        if i % 2 == 0:
            lambda_init = 0.8 - 0.6 * math.exp(-0.3 * i)
            z = _proj(h, even_w_in[j].astype(BF16), "even_in_proj").reshape(bsz, l, EVEN_IN)
            y_pool = _pool(z, pool_w[j].astype(BF16), row(pool_scale[j]))
            lam_params = jnp.stack([diff_lq1[j], diff_lk1[j], diff_lq2[j], diff_lk2[j]])
            y_attn = _attn(z, lam_params, diff_subln[j].reshape(-1, 1), lambda_init)
            ys = [y_pool.reshape(n, POOL_WIDTH), y_attn.reshape(n, DIFF_WIDTH)]
            routed = _mixout_route(ys, even_w_out[j].astype(BF16), h, row(ln_mix_g[i]), row(ln_mix_b[i]),
                                   router_w_t, router_bias, "even_out_proj")
        else:
            w_in = jnp.pad(odd_w_in[j], ((0, 0), (0, ODD_IN_PAD - ODD_IN))).astype(BF16)
            z = _proj(h, w_in, "odd_in_proj").reshape(bsz, l, ODD_IN_PAD)
            w2 = jnp.pad(gla_gate_w2[j], ((0, LANES - GLA_RANK), (0, 0))).astype(BF16)
            y = _gla(z, w2, row(gla_gate_b[j]), row(gla_head_norm[j]))
            routed = _mixout_route([y.reshape(n, GLA_V_WIDTH)], odd_w_out[j].astype(BF16), h,
                                   row(ln_mix_g[i]), row(ln_mix_b[i]), router_w_t, router_bias, "odd_out_proj")
        last = i == DEPTH - 1
        if last:
            out_idx_fn = lambda pos: pos.reshape(bsz, l)[:, PREFIX:].reshape(-1)
        else:
            out_idx_fn = lambda pos: pos
        h = _routed_moe(routed, moe_w_gate[i], moe_w_up[i], moe_w_down[i], row(ln_ffn_g[i]), row(ln_ffn_b[i]),
                        out_idx_fn, bsz * seq if last else n, f"moe_{i}")
    return h.reshape(bsz, seq, d)
```

```python
import functools
import math

import jax
import jax.numpy as jnp
from jax import lax
from jax.experimental import pallas as pl
from jax.experimental.pallas import tpu as pltpu

F32 = jnp.float32
BF16 = jnp.bfloat16

D_MODEL = 1024
DEPTH = 2
N_META = 16
PREFIX = 128
FIRST_VALID = PREFIX - N_META
POOL_WINDOWS = (2, 4, 8, 16)
POOL_GROUP = 128
POOL_WIDTH = 512
MAX_WINDOW = 16
DIFF_HEADS = 4
DIFF_QK_DIM = 64
DIFF_V_DIM = 128
DIFF_WIDTH = 512
EVEN_IN = 2048
GLA_HEADS = 4
GLA_DK = 128
GLA_DV = 256
GLA_RANK = 16
GLA_TAU = 16.0
GLA_QK_WIDTH = GLA_HEADS * GLA_DK
GLA_V_WIDTH = GLA_HEADS * GLA_DV
ODD_IN = 2 * GLA_QK_WIDTH + 2 * GLA_V_WIDTH + GLA_RANK
ODD_IN_PAD = 3200
N_EXPERTS = 16
N_GROUPS = 4
EXPERTS_PER_GROUP = 4
D_EXPERT = 512
TOP_K = 2
PAIRS = ((0, 1), (0, 2), (0, 3), (1, 2), (1, 3), (2, 3))
N_BUCKETS = N_GROUPS * len(PAIRS)
BUCKET_ROWS = 32
ALPHA = (2.0 * DEPTH) ** 0.25
NEG_BIG = -1e30
KT = 512
ONES_ROWS = 16

LANES = 128
VMEM_LIMIT = 56 * 1024 * 1024


def _pick(n, candidates):
    for c in candidates:
        if n % c == 0:
            return c
    raise ValueError(f"no tile for {n} in {candidates}")


def _cparams(sem):
    return pltpu.CompilerParams(dimension_semantics=sem, vmem_limit_bytes=VMEM_LIMIT)


def _layer_norm(a, g, b):
    mu = jnp.mean(a, axis=-1, keepdims=True)
    d = a - mu
    var = jnp.mean(d * d, axis=-1, keepdims=True)
    return d * lax.rsqrt(var + 1e-5) * g + b


def _proj_kernel(x_ref, w_ref, o_ref, *, tn):
    xb = x_ref[...].astype(BF16)
    for j in range(o_ref.shape[1] // tn):
        cols = slice(j * tn, (j + 1) * tn)
        o_ref[:, cols] = jnp.dot(xb, w_ref[:, cols], preferred_element_type=F32).astype(o_ref.dtype)


def _proj(x2d, w, name):
    n, k = x2d.shape
    wout = w.shape[1]
    tm = _pick(n, (640, 512, 256, 128))
    tn = _pick(wout, (640, 512, 128))
    return pl.pallas_call(
        functools.partial(_proj_kernel, tn=tn),
        grid=(n // tm,),
        in_specs=[pl.BlockSpec((tm, k), lambda i: (i, 0)),
                  pl.BlockSpec((k, wout), lambda i: (0, 0))],
        out_specs=pl.BlockSpec((tm, wout), lambda i: (i, 0)),
        out_shape=jax.ShapeDtypeStruct((n, wout), BF16),
        compiler_params=_cparams(("parallel",)),
        name=name,
    )(x2d, w)


def _pool_kernel(cur_ref, halo_ref, w_ref, sc_ref, o_ref, u_scr):
    t = pl.program_id(1)
    tl = cur_ref.shape[0]
    pos = t * tl + lax.broadcasted_iota(jnp.int32, (tl, 1), 0)
    hpos = t * tl - MAX_WINDOW + lax.broadcasted_iota(jnp.int32, (MAX_WINDOW, 1), 0)
    u_scr[0:MAX_WINDOW, :] = jnp.where(hpos >= FIRST_VALID, halo_ref[...].astype(F32), 0.0)
    u_scr[MAX_WINDOW:, :] = jnp.where(pos >= FIRST_VALID, cur_ref[...].astype(F32), 0.0)
    n_valid = pos - (FIRST_VALID - 1)
    for gi, w in enumerate(POOL_WINDOWS):
        cols = slice(gi * POOL_GROUP, (gi + 1) * POOL_GROUP)
        u = u_scr[MAX_WINDOW:MAX_WINDOW + tl, cols]
        win = u
        for j in range(1, w):
            win = win + u_scr[MAX_WINDOW - j:MAX_WINDOW - j + tl, cols]
        cnt = jnp.clip(n_valid, 1, w).astype(F32)
        d = win / cnt - u
        y = jnp.dot(d.astype(BF16), w_ref[gi], preferred_element_type=F32) * sc_ref[:, cols]
        o_ref[:, cols] = y.astype(o_ref.dtype)


def _pool(z, pool_w, pool_scale):
    b, l, _ = z.shape
    tl = _pick(l, (640, 128))
    hb = tl // MAX_WINDOW
    return pl.pallas_call(
        _pool_kernel,
        grid=(b, l // tl),
        in_specs=[pl.BlockSpec((None, tl, POOL_WIDTH), lambda bi, t: (bi, t, 0)),
                  pl.BlockSpec((None, MAX_WINDOW, POOL_WIDTH),
                               lambda bi, t: (bi, jnp.maximum(t * hb - 1, 0), 0)),
                  pl.BlockSpec((len(POOL_WINDOWS), POOL_GROUP, POOL_GROUP), lambda bi, t: (0, 0, 0)),
                  pl.BlockSpec((1, POOL_WIDTH), lambda bi, t: (0, 0))],
        out_specs=pl.BlockSpec((None, tl, POOL_WIDTH), lambda bi, t: (bi, t, 0)),
        out_shape=jax.ShapeDtypeStruct((b, l, POOL_WIDTH), BF16),
        scratch_shapes=[pltpu.VMEM((tl + MAX_WINDOW, POOL_WIDTH), F32)],
        compiler_params=_cparams(("parallel", "parallel")),
        name="pool_mixer",
    )(z, z, pool_w, pool_scale)


def _attn_kernel(lam_ref, q_ref, k_ref, v_ref, sub_ref, o_ref, ka_scr, kb_scr, vt_scr,
                 s0_scr, s1_scr, t0_scr, t1_scr, m0_scr, m1_scr, l0_scr, l1_scr, acc0_scr, acc1_scr,
                 *, tq, lambda_init):
    h = pl.program_id(1)
    i = pl.program_id(2)
    slope = jnp.exp2(-2.0 * (h + 1).astype(F32))
    half = DIFF_QK_DIM
    n_chunks = k_ref.shape[0] // LANES
    n_tail, n_pad = _attn_tail(tq)

    @pl.when(i == 0)
    def _():
        lane = lax.broadcasted_iota(jnp.int32, (LANES, 2 * half), 1)
        rowf = lax.broadcasted_iota(jnp.int32, (LANES, 2 * half), 0).astype(F32)

        def build(c, carry):
            rows = pl.ds(pl.multiple_of(c * LANES, LANES), LANES)
            kt = k_ref[rows, :].astype(F32)
            hi = jnp.full((LANES, 2 * half), c, jnp.int32).astype(F32)
            ka = jnp.where(lane < half, kt, jnp.where(lane == half, hi, jnp.where(lane == half + 1, rowf, 0.0)))
            kb = jnp.where(lane >= half, kt, jnp.where(lane == 0, hi, jnp.where(lane == 1, rowf, 0.0)))
            ka_scr[rows, :] = ka.astype(BF16)
            kb_scr[rows, :] = kb.astype(BF16)
            vt_scr[c, 0:DIFF_V_DIM, :] = v_ref[rows, :].astype(F32).T.astype(BF16)
            vt_scr[c, DIFF_V_DIM:, :] = jnp.ones((ONES_ROWS, LANES), BF16)
            return carry

        lax.fori_loop(0, n_chunks, build, 0)
        for extra in range(n_pad):
            pad_rows = pl.ds((n_chunks + extra) * LANES, LANES)
            ka_scr[pad_rows, :] = jnp.zeros((LANES, 2 * half), BF16)
            kb_scr[pad_rows, :] = jnp.zeros((LANES, 2 * half), BF16)
            vt_scr[n_chunks + extra] = jnp.zeros((DIFF_V_DIM + ONES_ROWS, LANES), BF16)

    lane = lax.broadcasted_iota(jnp.int32, (tq, 2 * half), 1)
    q = (q_ref[...] * jnp.asarray(half ** -0.5, BF16)).astype(F32)
    f_hi = LANES * slope
    qa = jnp.where(lane < half, q, jnp.where(lane == half, f_hi, jnp.where(lane == half + 1, slope, 0.0)))
    qb = jnp.where(lane >= half, q, jnp.where(lane == 0, f_hi, jnp.where(lane == 1, slope, 0.0)))
    q_maps = (qa.T.astype(BF16), qb.T.astype(BF16))
    k_maps = (ka_scr, kb_scr)
    s_scrs, t_scrs = (s0_scr, s1_scr), (t0_scr, t1_scr)
    m_scrs, l_scrs, acc_scrs = (m0_scr, m1_scr), (l0_scr, l1_scr), (acc0_scr, acc1_scr)
    for mp in range(2):
        m_scrs[mp][...] = jnp.full(m0_scr.shape, NEG_BIG, F32)
        l_scrs[mp][...] = jnp.zeros(l0_scr.shape, F32)
        acc_scrs[mp][...] = jnp.zeros(acc0_scr.shape, F32)
    q0 = i * tq
    n_full = q0 // KT

    def scores(t, mode):
        k0 = pl.multiple_of(t * KT, KT)
        buf = t % 2
        allowed = None
        if mode is not None:
            kpos = k0 + lax.broadcasted_iota(jnp.int32, (KT, tq), 0)
            allowed = kpos >= FIRST_VALID
            if mode == "diag":
                qpos = q0 + lax.broadcasted_iota(jnp.int32, (KT, tq), 1)
                allowed = jnp.logical_and(kpos <= qpos, jnp.logical_or(allowed, kpos == qpos))
        for mp in range(2):
            s = jnp.dot(k_maps[mp][pl.ds(k0, KT), :], q_maps[mp], preferred_element_type=F32)
            if allowed is not None:
                s = jnp.where(allowed, s, NEG_BIG)
            s_scrs[mp][buf] = s
            t_scrs[mp][buf] = jnp.max(s, axis=0, keepdims=True)

    def accumulate(t):
        c0 = t * (KT // LANES)
        buf = t % 2
        vt = jnp.concatenate([vt_scr[c0 + u] for u in range(KT // LANES)], axis=1)
        for mp in range(2):
            m_prev = m_scrs[mp][...]
            m_new = jnp.maximum(m_prev, t_scrs[mp][buf])
            alpha = jnp.exp(m_prev - m_new)
            p = jnp.exp(s_scrs[mp][buf] - m_new)
            pv = jnp.dot(vt, p.astype(BF16), preferred_element_type=F32)
            l_scrs[mp][...] = alpha * l_scrs[mp][...] + pv[DIFF_V_DIM:DIFF_V_DIM + 1, :]
            acc_scrs[mp][...] = alpha * acc_scrs[mp][...] + pv[0:DIFF_V_DIM, :]
            m_scrs[mp][...] = m_new

    @pl.when(n_full >= 1)
    def _():
        scores(0, "valid")

    @pl.when(n_full == 0)
    def _():
        scores(0, "diag")

    def steady(t, c):
        accumulate(t)
        scores(t + 1, None)
        return c

    lax.fori_loop(0, n_full - 1, steady, 0)

    @pl.when(n_full >= 1)
    def _():
        accumulate(n_full - 1)
        scores(n_full, "diag")

    for u in range(n_tail - 1):
        accumulate(n_full + u)
        scores(n_full + u + 1, "diag")
    accumulate(n_full + n_tail - 1)

    lam = (jnp.exp(jnp.sum(lam_ref[0:1, :] * lam_ref[1:2, :], axis=-1, keepdims=True))
           - jnp.exp(jnp.sum(lam_ref[2:3, :] * lam_ref[3:4, :], axis=-1, keepdims=True)) + lambda_init)
    o = acc0_scr[...] / l0_scr[...] - lam * (acc1_scr[...] / l1_scr[...])
    o = o * lax.rsqrt(jnp.mean(o * o, axis=0, keepdims=True) + 1e-6) * sub_ref[...] * (1.0 - lambda_init)
    o_ref[...] = o.T.astype(o_ref.dtype)


def _attn_tail(tq):
    n_tail = -(-(tq + KT - LANES) // KT)
    return n_tail, (n_tail * KT - tq) // LANES


def _attn(z, lam_params, subln_col, lambda_init):
    b, l, _ = z.shape
    tq = 5 * LANES
    assert l % tq == 0 and tq % LANES == 0 and KT % LANES == 0
    n_pad = _attn_tail(tq)[1]
    qb, kb, vb = POOL_WIDTH // LANES, (POOL_WIDTH + DIFF_WIDTH) // LANES, (POOL_WIDTH + 2 * DIFF_WIDTH) // LANES
    row = pltpu.VMEM((1, tq), F32)
    acc = pltpu.VMEM((DIFF_V_DIM, tq), F32)
    sbuf = pltpu.VMEM((2, KT, tq), F32)
    tbuf = pltpu.VMEM((2, 1, tq), F32)
    return pl.pallas_call(
        functools.partial(_attn_kernel, tq=tq, lambda_init=lambda_init),
        grid=(b, DIFF_HEADS, l // tq),
        in_specs=[pl.BlockSpec((4, DIFF_QK_DIM), lambda bi, h, i: (0, 0)),
                  pl.BlockSpec((None, tq, LANES), lambda bi, h, i: (bi, i, qb + h)),
                  pl.BlockSpec((None, l, LANES), lambda bi, h, i: (bi, 0, kb + h)),
                  pl.BlockSpec((None, l, LANES), lambda bi, h, i: (bi, 0, vb + h)),
                  pl.BlockSpec((DIFF_V_DIM, 1), lambda bi, h, i: (0, 0))],
        out_specs=pl.BlockSpec((None, tq, DIFF_V_DIM), lambda bi, h, i: (bi, i, h)),
        out_shape=jax.ShapeDtypeStruct((b, l, DIFF_WIDTH), BF16),
        scratch_shapes=[pltpu.VMEM((l + n_pad * LANES, LANES), BF16), pltpu.VMEM((l + n_pad * LANES, LANES), BF16),
                        pltpu.VMEM((l // LANES + n_pad, DIFF_V_DIM + ONES_ROWS, LANES), BF16),
                        sbuf, sbuf, tbuf, tbuf, row, row, row, row, acc, acc],
        compiler_params=_cparams(("parallel", "parallel", "arbitrary")),
        name="diff_attn",
    )(lam_params, z, z, z, subln_col)


def _gla_kernel(q_ref, k_ref, v_ref, r_ref, glr_ref, w2_ref, gb_ref, hn_ref, o_ref, st_ref, *, c_len, sb):
    c = pl.program_id(0)
    n_batch = q_ref.shape[0]

    @pl.when(c == 0)
    def _():
        st_ref[...] = jnp.zeros(st_ref.shape, F32)

    pos = c * c_len + lax.broadcasted_iota(jnp.int32, (c_len, 1), 0)
    validf = (pos >= FIRST_VALID).astype(F32)
    ri = lax.broadcasted_iota(jnp.int32, (c_len, c_len), 0)
    ci = lax.broadcasted_iota(jnp.int32, (c_len, c_len), 1)
    tri = jnp.where(ri >= ci, 1.0, 0.0).astype(BF16)
    row = lax.broadcasted_iota(jnp.int32, (c_len, 1), 0)
    t_loc = lax.broadcasted_iota(jnp.int32, (sb, 1), 0)
    lane_c = lax.broadcasted_iota(jnp.int32, (sb, c_len), 1)
    nt = (((1,), (1,)), ((), ()))
    tn = (((0,), (0,)), ((), ()))

    for bi, h in [(bi, h) for bi in range(n_batch) for h in range(GLA_HEADS)]:
        glr = glr_ref[bi]
        ks = slice(h * GLA_DK, (h + 1) * GLA_DK)
        vs = slice(h * GLA_DV, (h + 1) * GLA_DV)
        g = jnp.dot(glr, w2_ref[:, ks], preferred_element_type=F32) + gb_ref[:, ks]
        log_a = -(jnp.maximum(-g, 0.0) + jnp.log(1.0 + jnp.exp(-jnp.abs(g)))) * (1.0 / GLA_TAU)
        la_hi = log_a.astype(BF16)
        la_lo = (log_a - la_hi.astype(F32)).astype(BF16)
        b = (jnp.dot(tri, la_hi, preferred_element_type=F32)
             + jnp.dot(tri, la_lo, preferred_element_type=F32))
        q = q_ref[bi, :, ks].astype(F32) * (GLA_DK ** -0.5)
        k = k_ref[bi, :, ks].astype(F32) * validf
        v = v_ref[bi, :, vs]
        st = st_ref[bi, h]

        blocks = []
        for i in range(c_len // sb):
            lo = i * sb
            q_i = q[lo:lo + sb]
            b_i = b[lo:lo + sb]
            a_i = jnp.zeros((sb, c_len), F32)
            if i > 0:
                b_ref_row = b[lo - 1:lo]
                q_t = q_i * jnp.exp(b_i - b_ref_row)
                k_t = jnp.where(row < lo, k * jnp.exp(jnp.minimum(b_ref_row - b, 0.0)), 0.0)
                a_i = lax.dot_general(q_t.astype(BF16), k_t.astype(BF16), nt, preferred_element_type=F32)
            for s in range(sb):
                r = lo + s
                e = jnp.exp(b_i - b[r:r + 1])
                col = jnp.sum(q_i * k[r:r + 1] * e, axis=-1, keepdims=True)
                col = jnp.where(t_loc >= s, col, 0.0)
                a_i = jnp.where(lane_c == r, col, a_i)
            blocks.append(a_i)
        att = jnp.concatenate(blocks, axis=0)

        o = jnp.dot(att.astype(BF16), v, preferred_element_type=F32)
        o = o + lax.dot_general((q * jnp.exp(b)).astype(BF16), st.astype(BF16), nt,
                                preferred_element_type=F32)
        b_last = b[c_len - 1:c_len]
        k_hat = (k * jnp.exp(b_last - b)).astype(BF16)
        st_ref[bi, h] = st * jnp.exp(b_last) + lax.dot_general(v, k_hat, tn, preferred_element_type=F32)

        o = o * lax.rsqrt(jnp.mean(o * o, axis=-1, keepdims=True) + 1e-6) * hn_ref[...]
        rg = r_ref[bi, :, vs].astype(F32)
        o_ref[bi, :, vs] = (o * (rg * jax.nn.sigmoid(rg))).astype(o_ref.dtype)


def _gla(z, w2, gate_b, head_norm):
    b, l, _ = z.shape
    c_len = 128
    qw, vw = GLA_QK_WIDTH, GLA_V_WIDTH
    return pl.pallas_call(
        functools.partial(_gla_kernel, c_len=c_len, sb=16),
        grid=(l // c_len,),
        in_specs=[pl.BlockSpec((b, c_len, qw), lambda c: (0, c, 0)),
                  pl.BlockSpec((b, c_len, qw), lambda c: (0, c, 1)),
                  pl.BlockSpec((b, c_len, vw), lambda c: (0, c, 1)),
                  pl.BlockSpec((b, c_len, vw), lambda c: (0, c, 2)),
                  pl.BlockSpec((b, c_len, LANES), lambda c: (0, c, (2 * qw + 2 * vw) // LANES)),
                  pl.BlockSpec((LANES, qw), lambda c: (0, 0)),
                  pl.BlockSpec((1, qw), lambda c: (0, 0)),
                  pl.BlockSpec((1, GLA_DV), lambda c: (0, 0))],
        out_specs=pl.BlockSpec((b, c_len, vw), lambda c: (0, c, 0)),
        out_shape=jax.ShapeDtypeStruct((b, l, vw), BF16),
        scratch_shapes=[pltpu.VMEM((b, GLA_HEADS, GLA_DV, GLA_DK), F32)],
        compiler_params=_cparams(("arbitrary",)),
        name="gla_mixer",
    )(z, z, z, z, z, w2, gate_b, head_norm)


def _route_tokens(x, bias_ref, rw_ref, haug_ref, bucket_ref, rank_ref, cnt_ref, upper_scr, cnt_scr):
    tm, d = x.shape

    @pl.when(pl.program_id(0) == 0)
    def _():
        r = lax.broadcasted_iota(jnp.int32, (tm, tm), 0)
        c = lax.broadcasted_iota(jnp.int32, (tm, tm), 1)
        upper_scr[...] = jnp.where(r < c, 1.0, 0.0).astype(BF16)
        cnt_scr[...] = jnp.zeros(cnt_scr.shape, F32)

    w = rw_ref[...]
    xh = x.astype(BF16)
    xl = (x - xh.astype(F32)).astype(BF16)
    wh = w.astype(BF16)
    wl = (w - wh.astype(F32)).astype(BF16)
    nt = (((1,), (1,)), ((), ()))
    logits = (lax.dot_general(wh, xh, nt, preferred_element_type=F32)
              + lax.dot_general(wh, xl, nt, preferred_element_type=F32)
              + lax.dot_general(wl, xh, nt, preferred_element_type=F32))
    lg = [logits[e:e + 1, :] for e in range(N_EXPERTS)]
    mx = functools.reduce(jnp.maximum, lg)
    ex = [jnp.exp(v - mx) for v in lg]
    den = functools.reduce(jnp.add, ex)
    probs = [v / den for v in ex]
    sel = [probs[e] + bias_ref[e] for e in range(N_EXPERTS)]

    def top2_sum(a, b, c, d_):
        hi1, lo1 = jnp.maximum(a, b), jnp.minimum(a, b)
        hi2, lo2 = jnp.maximum(c, d_), jnp.minimum(c, d_)
        return jnp.maximum(hi1, hi2) + jnp.maximum(jnp.minimum(hi1, hi2), jnp.maximum(lo1, lo2))

    best = top2_sum(*sel[0:4])
    gidx = jnp.zeros((1, tm), jnp.int32)
    for g in range(1, N_GROUPS):
        sc = top2_sum(*sel[4 * g:4 * g + 4])
        better = sc > best
        gidx = jnp.where(better, g, gidx)
        best = jnp.maximum(best, sc)

    def pick(vals, j):
        out = vals[j]
        for g in range(1, N_GROUPS):
            out = jnp.where(gidx == g, vals[4 * g + j], out)
        return out

    sg = [pick(sel, j) for j in range(EXPERTS_PER_GROUP)]
    pg = [pick(probs, j) for j in range(EXPERTS_PER_GROUP)]
    picked = []
    for j in range(EXPERTS_PER_GROUP):
        rank = jnp.zeros((1, tm), jnp.int32)
        for o in range(EXPERTS_PER_GROUP):
            if o == j:
                continue
            ahead = (sg[o] > sg[j]) if o > j else (sg[o] >= sg[j])
            rank = rank + ahead.astype(jnp.int32)
        picked.append(rank < TOP_K)
    chosen = [jnp.where(picked[j], pg[j], 0.0) for j in range(EXPERTS_PER_GROUP)]
    wsum = functools.reduce(jnp.add, chosen)
    wn = [cj / wsum for cj in chosen]

    pair = jnp.where(picked[0],
                     jnp.where(picked[1], 0, jnp.where(picked[2], 1, 2)),
                     jnp.where(picked[1], jnp.where(picked[2], 3, 4), 5))
    w_lo = jnp.where(picked[0], wn[0], jnp.where(picked[1], wn[1], wn[2]))
    w_hi = jnp.where(picked[3], wn[3], jnp.where(picked[2], wn[2], wn[1]))
    bucket = gidx * len(PAIRS) + pair

    onehot = jnp.where(lax.broadcasted_iota(jnp.int32, (BUCKET_ROWS, tm), 0) == bucket, 1.0, 0.0)
    before = jnp.dot(onehot.astype(BF16), upper_scr[...], preferred_element_type=F32)
    seen = before + cnt_scr[:, 0:1]
    bucket_ref[...] = bucket
    rank_ref[...] = jnp.sum(onehot * seen, axis=0, keepdims=True).astype(jnp.int32)
    cnt_scr[...] = cnt_scr[...] + jnp.sum(onehot, axis=1, keepdims=True)
    cnt_ref[...] = cnt_scr[...]

    pay_t = jnp.concatenate([w_lo, w_hi, jnp.zeros((LANES - TOP_K, tm), F32)], axis=0)
    haug_ref[:, 0:d] = x
    haug_ref[:, d:] = pay_t.T


def _mixout_route_kernel(*refs):
    bias_ref, *y_refs, w_ref, h_ref, g_ref, b_ref, rw_ref = refs[:-6]
    acc = ALPHA * h_ref[...]
    off = 0
    for y_ref in y_refs:
        kd = y_ref.shape[1]
        acc = acc + jnp.dot(y_ref[...], w_ref[off:off + kd, :], preferred_element_type=F32)
        off += kd
    _route_tokens(_layer_norm(acc, g_ref[...], b_ref[...]), bias_ref, rw_ref, *refs[-6:])


def _mixout_route(ys, w, h2d, g, b, router_w_t, router_bias, name):
    n, d = h2d.shape
    tm = _pick(n, (640, 512, 256, 128))
    rows = lambda width: pl.BlockSpec((tm, width), lambda i: (i, 0))
    whole = lambda a: pl.BlockSpec(a.shape, lambda i: (0, 0))
    per_tile = pl.BlockSpec((None, 1, tm), lambda i: (i, 0, 0))
    return pl.pallas_call(
        _mixout_route_kernel,
        grid=(n // tm,),
        in_specs=[pl.BlockSpec(memory_space=pltpu.SMEM)] + [rows(y.shape[1]) for y in ys]
        + [whole(w), rows(d), whole(g), whole(b), whole(router_w_t)],
        out_specs=[rows(d + LANES), per_tile, per_tile, pl.BlockSpec((BUCKET_ROWS, LANES), lambda i: (0, 0))],
        out_shape=[jax.ShapeDtypeStruct((n, d + LANES), F32),
                   jax.ShapeDtypeStruct((n // tm, 1, tm), jnp.int32),
                   jax.ShapeDtypeStruct((n // tm, 1, tm), jnp.int32),
                   jax.ShapeDtypeStruct((BUCKET_ROWS, LANES), F32)],
        scratch_shapes=[pltpu.VMEM((tm, tm), BF16), pltpu.VMEM((BUCKET_ROWS, LANES), F32)],
        compiler_params=_cparams(("arbitrary",)),
        name=name,
    )(router_bias, *ys, w, h2d, g, b, router_w_t)


def _row_dma(src_ref, dst_ref, sem, src_row, dst_row):
    return pltpu.make_async_copy(src_ref.at[pl.ds(src_row, 1), :], dst_ref.at[pl.ds(dst_row, 1), :], sem)


def _scatter_rows_kernel(idx_ref, src_ref, dst_ref, sem, *, rows):
    base = pl.program_id(0) * rows
    for r in range(rows):
        _row_dma(src_ref, dst_ref, sem, r, idx_ref[base + r]).start(priority=r % 2)

    def drain(r, c):
        _row_dma(src_ref, dst_ref, sem, 0, 0).wait()
        return c

    lax.fori_loop(0, rows, drain, 0, unroll=8)


def _gather_rows_kernel(idx_ref, src_ref, dst_ref, sem, *, rows):
    base = pl.program_id(0) * rows
    for r in range(rows):
        _row_dma(src_ref, dst_ref, sem, idx_ref[base + r], r).start(priority=r % 2)

    def drain(r, c):
        _row_dma(src_ref, dst_ref, sem, 0, 0).wait()
        return c

    lax.fori_loop(0, rows, drain, 0, unroll=8)


def _permute_rows(body, idx, src, n_dst, name):
    m = idx.shape[0]
    width = src.shape[1]
    rows = _pick(m, (512, 256, 128))
    tile = pl.BlockSpec((rows, width), lambda i, idx_ref: (i, 0))
    whole = pl.BlockSpec(memory_space=pl.ANY)
    scatter = body is _scatter_rows_kernel
    return pl.pallas_call(
        functools.partial(body, rows=rows),
        grid_spec=pltpu.PrefetchScalarGridSpec(
            num_scalar_prefetch=1,
            grid=(m // rows,),
            in_specs=[tile if scatter else whole],
            out_specs=whole if scatter else tile,
            scratch_shapes=[pltpu.SemaphoreType.DMA(())]),
        out_shape=jax.ShapeDtypeStruct((n_dst, width), src.dtype),
        compiler_params=_cparams(("arbitrary",)),
        name=name,
    )(idx, src)


def _moe_kernel(blk_ref, e_lo_ref, e_hi_ref, new_lo_ref, new_hi_ref, nv_ref, x_ref,
                wg_lo, wg_hi, wu_lo, wu_hi, wd_lo, wd_hi, g_ref, b_ref, o_ref, wg_scr, wu_scr, wd_scr):
    t = pl.program_id(0)
    nv = nv_ref[t]

    for j, (flag_ref, wg, wu, wd) in enumerate(((new_lo_ref, wg_lo, wu_lo, wd_lo), (new_hi_ref, wg_hi, wu_hi, wd_hi))):
        @pl.when(flag_ref[t] > 0)
        def _():
            wg_scr[j] = wg[...].astype(BF16)
            wu_scr[j] = wu[...].astype(BF16)
            wd_scr[j] = wd[...].astype(BF16)

    @pl.when(nv > 0)
    def _():
        tm, d = o_ref.shape
        live = lax.broadcasted_iota(jnp.int32, (tm, 1), 0) < nv
        x = jnp.where(live, x_ref[:, 0:d], 0.0)
        cw = jnp.where(live, x_ref[:, d:], 0.0)
        xb = x.astype(BF16)
        acc = ALPHA * x
        for j in range(TOP_K):
            gate = jnp.dot(xb, wg_scr[j], preferred_element_type=F32)
            up = jnp.dot(xb, wu_scr[j], preferred_element_type=F32)
            act = gate * jax.nn.sigmoid(gate) * up * cw[:, j:j + 1]
            acc = acc + jnp.dot(act.astype(BF16), wd_scr[j], preferred_element_type=F32)
        o_ref[...] = _layer_norm(acc, g_ref[...], b_ref[...])


def _moe(x_sorted, tables, wg, wu, wd, layer, g, b, tm, name):
    rows, wa = x_sorted.shape
    d = wa - LANES
    n_tiles = tables[0].shape[0]
    rows_of = lambda t, blk, e_lo, e_hi, new_lo, new_hi, nv: (blk[t], 0)
    lo = lambda t, blk, e_lo, e_hi, new_lo, new_hi, nv: (layer, e_lo[t], 0, 0)
    hi = lambda t, blk, e_lo, e_hi, new_lo, new_hi, nv: (layer, e_hi[t], 0, 0)
    const = lambda t, blk, e_lo, e_hi, new_lo, new_hi, nv: (0, 0)
    w_in = lambda f: pl.BlockSpec((None, None, d, D_EXPERT), f)
    w_out = lambda f: pl.BlockSpec((None, None, D_EXPERT, d), f)
    return pl.pallas_call(
        _moe_kernel,
        grid_spec=pltpu.PrefetchScalarGridSpec(
            num_scalar_prefetch=6,
            grid=(n_tiles,),
            in_specs=[pl.BlockSpec((tm, wa), rows_of),
                      w_in(lo), w_in(hi), w_in(lo), w_in(hi), w_out(lo), w_out(hi),
                      pl.BlockSpec((1, d), const), pl.BlockSpec((1, d), const)],
            out_specs=pl.BlockSpec((tm, d), rows_of),
            scratch_shapes=[pltpu.VMEM((TOP_K, d, D_EXPERT), BF16), pltpu.VMEM((TOP_K, d, D_EXPERT), BF16),
                            pltpu.VMEM((TOP_K, D_EXPERT, d), BF16)]),
        out_shape=jax.ShapeDtypeStruct((rows, d), F32),
        compiler_params=_cparams(("arbitrary",)),
        name=name,
    )(*tables, x_sorted, wg, wg, wu, wu, wd, wd, g, b)


def _bucket_layout(counts, tm, n_tiles):
    tiles_b = (counts + tm - 1) // tm
    ends = jnp.cumsum(tiles_b)
    starts = ends - tiles_b
    t = jnp.arange(n_tiles, dtype=jnp.int32)
    tc = jnp.minimum(t, ends[-1] - 1)
    bkt = jnp.sum((tc[:, None] >= ends[None, :]).astype(jnp.int32), axis=1)
    nv = jnp.where(t < ends[-1], jnp.clip(counts[bkt] - (tc - starts[bkt]) * tm, 0, tm), 0)
    pair_lo = jnp.array([p[0] for p in PAIRS], jnp.int32)
    pair_hi = jnp.array([p[1] for p in PAIRS], jnp.int32)
    grp, pair = bkt // len(PAIRS), bkt % len(PAIRS)
    e_lo = grp * EXPERTS_PER_GROUP + pair_lo[pair]
    e_hi = grp * EXPERTS_PER_GROUP + pair_hi[pair]
    changed = lambda e: jnp.concatenate([jnp.ones((1,), jnp.int32), (e[1:] != e[:-1]).astype(jnp.int32)])
    i32 = lambda a: a.astype(jnp.int32)
    return i32(starts * tm), (i32(tc), i32(e_lo), i32(e_hi), changed(e_lo), changed(e_hi), i32(nv))


def _routed_moe(routed, wg, wu, wd, layer, g, b, out_idx_fn, n_out, name):
    haug, bucket, rank, cnt = routed
    n = haug.shape[0]
    tm = _pick(n, (256, 128))
    n_tiles = n // tm + N_BUCKETS
    counts = cnt[:N_BUCKETS, 0].astype(jnp.int32)
    row0, tables = _bucket_layout(counts, tm, n_tiles)
    pos = row0[bucket.reshape(n)] + rank.reshape(n)
    x_sorted = _permute_rows(_scatter_rows_kernel, pos, haug, n_tiles * tm, name + "_scatter")
    y_sorted = _moe(x_sorted, tables, wg, wu, wd, layer, g, b, tm, name)
    return _permute_rows(_gather_rows_kernel, out_idx_fn(pos), y_sorted, n_out, name + "_gather")


def kernel(x, meta, even_w_in, pool_w, pool_scale, diff_lq1, diff_lk1, diff_lq2, diff_lk2, diff_subln,
           even_w_out, odd_w_in, gla_gate_w2, gla_gate_b, gla_head_norm, odd_w_out, ln_mix_g, ln_mix_b,
           ln_ffn_g, ln_ffn_b, router_w, router_bias, moe_w_gate, moe_w_up, moe_w_down):
    bsz, seq, d = x.shape
    l = PREFIX + seq
    n = bsz * l
    pad = jnp.zeros((bsz, FIRST_VALID, d), x.dtype)
    metas = jnp.broadcast_to(meta.astype(x.dtype)[None], (bsz, N_META, d))
    h = jnp.concatenate([pad, metas, x], axis=1).reshape(n, d)
    router_w_t = router_w.T
    row = lambda a: a.reshape(1, -1)

    for i in range(DEPTH):
        j = i // 2
        if i % 2 == 0:
            lambda_init = 0.8 - 0.6 * math.exp(-0.3 * i)
            z = _proj(h, even_w_in[j].astype(BF16), "even_in_proj").reshape(bsz, l, EVEN_IN)
            y_pool = _pool(z, pool_w[j].astype(BF16), row(pool_scale[j]))
            lam_params = jnp.stack([diff_lq1[j], diff_lk1[j], diff_lq2[j], diff_lk2[j]])
            y_attn = _attn(z, lam_params, diff_subln[j].reshape(-1, 1), lambda_init)
            ys = [y_pool.reshape(n, POOL_WIDTH), y_attn.reshape(n, DIFF_WIDTH)]
            routed = _mixout_route(ys, even_w_out[j].astype(BF16), h, row(ln_mix_g[i]), row(ln_mix_b[i]),
                                   router_w_t, router_bias, "even_out_proj")
        else:
            w_in = jnp.pad(odd_w_in[j], ((0, 0), (0, ODD_IN_PAD - ODD_IN))).astype(BF16)
            z = _proj(h, w_in, "odd_in_proj").reshape(bsz, l, ODD_IN_PAD)
            w2 = jnp.pad(gla_gate_w2[j], ((0, LANES - GLA_RANK), (0, 0))).astype(BF16)
            y = _gla(z, w2, row(gla_gate_b[j]), row(gla_head_norm[j]))
            routed = _mixout_route([y.reshape(n, GLA_V_WIDTH)], odd_w_out[j].astype(BF16), h,
                                   row(ln_mix_g[i]), row(ln_mix_b[i]), router_w_t, router_bias, "odd_out_proj")
        last = i == DEPTH - 1
        if last:
            out_idx_fn = lambda pos: pos.reshape(bsz, l)[:, PREFIX:].reshape(-1)
        else:
            out_idx_fn = lambda pos: pos
        h = _routed_moe(routed, moe_w_gate, moe_w_up, moe_w_down, i, row(ln_ffn_g[i]), row(ln_ffn_b[i]),
                        out_idx_fn, bsz * seq if last else n, f"moe_{i}")
    return h.reshape(bsz, seq, d)
```

```python
import functools
import math

import jax
import jax.numpy as jnp
from jax import lax
from jax.experimental import pallas as pl
from jax.experimental.pallas import tpu as pltpu

F32 = jnp.float32
BF16 = jnp.bfloat16

D_MODEL = 1024
DEPTH = 2
N_META = 16
PREFIX = 128
FIRST_VALID = PREFIX - N_META
POOL_WINDOWS = (2, 4, 8, 16)
POOL_GROUP = 128
POOL_WIDTH = 512
MAX_WINDOW = 16
DIFF_HEADS = 4
DIFF_QK_DIM = 64
DIFF_V_DIM = 128
DIFF_WIDTH = 512
EVEN_IN = 2048
GLA_HEADS = 4
GLA_DK = 128
GLA_DV = 256
GLA_RANK = 16
GLA_TAU = 16.0
GLA_QK_WIDTH = GLA_HEADS * GLA_DK
GLA_V_WIDTH = GLA_HEADS * GLA_DV
ODD_IN = 2 * GLA_QK_WIDTH + 2 * GLA_V_WIDTH + GLA_RANK
ODD_IN_PAD = 3200
N_EXPERTS = 16
N_GROUPS = 4
EXPERTS_PER_GROUP = 4
D_EXPERT = 512
TOP_K = 2
PAIRS = ((0, 1), (0, 2), (0, 3), (1, 2), (1, 3), (2, 3))
N_BUCKETS = N_GROUPS * len(PAIRS)
BUCKET_ROWS = 32
ALPHA = (2.0 * DEPTH) ** 0.25
NEG_BIG = -1e30
LOG2E = math.log2(math.e)
KT = 640
ONES_ROWS = 16

LANES = 128
VMEM_LIMIT = 56 * 1024 * 1024


def _pick(n, candidates):
    for c in candidates:
        if n % c == 0:
            return c
    raise ValueError(f"no tile for {n} in {candidates}")


def _cparams(sem):
    return pltpu.CompilerParams(dimension_semantics=sem, vmem_limit_bytes=VMEM_LIMIT)


def _layer_norm(a, g, b):
    mu = jnp.mean(a, axis=-1, keepdims=True)
    d = a - mu
    var = jnp.mean(d * d, axis=-1, keepdims=True)
    return d * lax.rsqrt(var + 1e-5) * g + b


def _proj_kernel(x_ref, w_ref, o_ref, *, tn):
    xb = x_ref[...].astype(BF16)
    for j in range(o_ref.shape[1] // tn):
        cols = slice(j * tn, (j + 1) * tn)
        o_ref[:, cols] = jnp.dot(xb, w_ref[:, cols], preferred_element_type=F32).astype(o_ref.dtype)


def _proj(x2d, w, name):
    n, k = x2d.shape
    wout = w.shape[1]
    tm = _pick(n, (640, 512, 256, 128))
    tn = _pick(wout, (640, 512, 128))
    return pl.pallas_call(
        functools.partial(_proj_kernel, tn=tn),
        grid=(n // tm,),
        in_specs=[pl.BlockSpec((tm, k), lambda i: (i, 0)),
                  pl.BlockSpec((k, wout), lambda i: (0, 0))],
        out_specs=pl.BlockSpec((tm, wout), lambda i: (i, 0)),
        out_shape=jax.ShapeDtypeStruct((n, wout), BF16),
        compiler_params=_cparams(("parallel",)),
        name=name,
    )(x2d, w)


def _pool_kernel(cur_ref, halo_ref, w_ref, sc_ref, o_ref, u_scr):
    t = pl.program_id(1)
    tl = cur_ref.shape[0]
    pos = t * tl + lax.broadcasted_iota(jnp.int32, (tl, 1), 0)
    hpos = t * tl - MAX_WINDOW + lax.broadcasted_iota(jnp.int32, (MAX_WINDOW, 1), 0)
    u_scr[0:MAX_WINDOW, :] = jnp.where(hpos >= FIRST_VALID, halo_ref[...].astype(F32), 0.0)
    u_scr[MAX_WINDOW:, :] = jnp.where(pos >= FIRST_VALID, cur_ref[...].astype(F32), 0.0)
    n_valid = pos - (FIRST_VALID - 1)
    for gi, w in enumerate(POOL_WINDOWS):
        cols = slice(gi * POOL_GROUP, (gi + 1) * POOL_GROUP)
        u = u_scr[MAX_WINDOW:MAX_WINDOW + tl, cols]
        win = u
        for j in range(1, w):
            win = win + u_scr[MAX_WINDOW - j:MAX_WINDOW - j + tl, cols]
        cnt = jnp.clip(n_valid, 1, w).astype(F32)
        d = win / cnt - u
        y = jnp.dot(d.astype(BF16), w_ref[gi], preferred_element_type=F32) * sc_ref[:, cols]
        o_ref[:, cols] = y.astype(o_ref.dtype)


def _pool(z, pool_w, pool_scale):
    b, l, _ = z.shape
    tl = _pick(l, (640, 128))
    hb = tl // MAX_WINDOW
    return pl.pallas_call(
        _pool_kernel,
        grid=(b, l // tl),
        in_specs=[pl.BlockSpec((None, tl, POOL_WIDTH), lambda bi, t: (bi, t, 0)),
                  pl.BlockSpec((None, MAX_WINDOW, POOL_WIDTH),
                               lambda bi, t: (bi, jnp.maximum(t * hb - 1, 0), 0)),
                  pl.BlockSpec((len(POOL_WINDOWS), POOL_GROUP, POOL_GROUP), lambda bi, t: (0, 0, 0)),
                  pl.BlockSpec((1, POOL_WIDTH), lambda bi, t: (0, 0))],
        out_specs=pl.BlockSpec((None, tl, POOL_WIDTH), lambda bi, t: (bi, t, 0)),
        out_shape=jax.ShapeDtypeStruct((b, l, POOL_WIDTH), BF16),
        scratch_shapes=[pltpu.VMEM((tl + MAX_WINDOW, POOL_WIDTH), F32)],
        compiler_params=_cparams(("parallel", "parallel")),
        name="pool_mixer",
    )(z, z, pool_w, pool_scale)


def _attn_kernel(lam_ref, q_ref, k_ref, v_ref, sub_ref, o_ref, ka_scr, kb_scr, vt_scr,
                 s0_scr, s1_scr, t0_scr, t1_scr, m0_scr, m1_scr, l0_scr, l1_scr, acc0_scr, acc1_scr,
                 *, tq, lambda_init):
    h = pl.program_id(1)
    i = pl.program_id(2)
    slope = jnp.exp2(-2.0 * (h + 1).astype(F32))
    half = DIFF_QK_DIM
    n_chunks = k_ref.shape[0] // LANES
    n_tail, n_pad = _attn_tail(tq)

    @pl.when(i == 0)
    def _():
        lane = lax.broadcasted_iota(jnp.int32, (LANES, 2 * half), 1)
        rowf = lax.broadcasted_iota(jnp.int32, (LANES, 2 * half), 0).astype(F32)

        def build(c, carry):
            rows = pl.ds(pl.multiple_of(c * LANES, LANES), LANES)
            kt = k_ref[rows, :].astype(F32)
            hi = jnp.full((LANES, 2 * half), c, jnp.int32).astype(F32)
            ka = jnp.where(lane < half, kt, jnp.where(lane == half, hi, jnp.where(lane == half + 1, rowf, 0.0)))
            kb = jnp.where(lane >= half, kt, jnp.where(lane == 0, hi, jnp.where(lane == 1, rowf, 0.0)))
            ka_scr[rows, :] = ka.astype(BF16)
            kb_scr[rows, :] = kb.astype(BF16)
            vt_scr[c, 0:DIFF_V_DIM, :] = v_ref[rows, :].astype(F32).T.astype(BF16)
            vt_scr[c, DIFF_V_DIM:, :] = jnp.ones((ONES_ROWS, LANES), BF16)
            return carry

        lax.fori_loop(0, n_chunks, build, 0)
        for extra in range(n_pad):
            pad_rows = pl.ds((n_chunks + extra) * LANES, LANES)
            ka_scr[pad_rows, :] = jnp.zeros((LANES, 2 * half), BF16)
            kb_scr[pad_rows, :] = jnp.zeros((LANES, 2 * half), BF16)
            vt_scr[n_chunks + extra] = jnp.zeros((DIFF_V_DIM + ONES_ROWS, LANES), BF16)

    lane = lax.broadcasted_iota(jnp.int32, (tq, 2 * half), 1)
    q = (q_ref[...] * jnp.asarray(half ** -0.5, BF16)).astype(F32)
    f_hi = LANES * slope
    qa = jnp.where(lane < half, q, jnp.where(lane == half, f_hi, jnp.where(lane == half + 1, slope, 0.0)))
    qb = jnp.where(lane >= half, q, jnp.where(lane == 0, f_hi, jnp.where(lane == 1, slope, 0.0)))
    q_maps = (qa.T.astype(BF16), qb.T.astype(BF16))
    k_maps = (ka_scr, kb_scr)
    s_scrs, t_scrs = (s0_scr, s1_scr), (t0_scr, t1_scr)
    m_scrs, l_scrs, acc_scrs = (m0_scr, m1_scr), (l0_scr, l1_scr), (acc0_scr, acc1_scr)
    for mp in range(2):
        m_scrs[mp][...] = jnp.full(m0_scr.shape, NEG_BIG, F32)
        l_scrs[mp][...] = jnp.zeros(l0_scr.shape, F32)
        acc_scrs[mp][...] = jnp.zeros(acc0_scr.shape, F32)
    q0 = i * tq
    n_full = q0 // KT

    def scores(t, mode):
        k0 = pl.multiple_of(t * KT, KT)
        buf = t % 2
        allowed = None
        if mode is not None:
            kpos = k0 + lax.broadcasted_iota(jnp.int32, (KT, tq), 0)
            allowed = kpos >= FIRST_VALID
            if mode == "diag":
                qpos = q0 + lax.broadcasted_iota(jnp.int32, (KT, tq), 1)
                allowed = jnp.logical_and(kpos <= qpos, jnp.logical_or(allowed, kpos == qpos))
        for mp in range(2):
            s = jnp.dot(k_maps[mp][pl.ds(k0, KT), :], q_maps[mp], preferred_element_type=F32)
            if allowed is not None:
                s = jnp.where(allowed, s, NEG_BIG)
            s_scrs[mp][buf] = s
            t_scrs[mp][buf] = jnp.max(s, axis=0, keepdims=True)

    def accumulate(t):
        c0 = t * (KT // LANES)
        buf = t % 2
        vt = jnp.concatenate([vt_scr[c0 + u] for u in range(KT // LANES)], axis=1)
        for mp in range(2):
            m_prev = m_scrs[mp][...]
            m_new = jnp.maximum(m_prev, t_scrs[mp][buf])
            alpha = jnp.exp(m_prev - m_new)
            p = jnp.exp(s_scrs[mp][buf] - m_new)
            pv = jnp.dot(vt, p.astype(BF16), preferred_element_type=F32)
            l_scrs[mp][...] = alpha * l_scrs[mp][...] + pv[DIFF_V_DIM:DIFF_V_DIM + 1, :]
            acc_scrs[mp][...] = alpha * acc_scrs[mp][...] + pv[0:DIFF_V_DIM, :]
            m_scrs[mp][...] = m_new

    @pl.when(n_full >= 1)
    def _():
        scores(0, "valid")

    @pl.when(n_full == 0)
    def _():
        scores(0, "diag")

    def steady(t, c):
        accumulate(t)
        scores(t + 1, None)
        return c

    lax.fori_loop(0, n_full - 1, steady, 0)

    @pl.when(n_full >= 1)
    def _():
        accumulate(n_full - 1)
        scores(n_full, "diag")

    for u in range(n_tail - 1):
        accumulate(n_full + u)
        scores(n_full + u + 1, "diag")
    accumulate(n_full + n_tail - 1)

    lam = (jnp.exp(jnp.sum(lam_ref[0:1, :] * lam_ref[1:2, :], axis=-1, keepdims=True))
           - jnp.exp(jnp.sum(lam_ref[2:3, :] * lam_ref[3:4, :], axis=-1, keepdims=True)) + lambda_init)
    o = acc0_scr[...] / l0_scr[...] - lam * (acc1_scr[...] / l1_scr[...])
    o = o * lax.rsqrt(jnp.mean(o * o, axis=0, keepdims=True) + 1e-6) * sub_ref[...] * (1.0 - lambda_init)
    o_ref[...] = o.T.astype(o_ref.dtype)


def _attn_tail(tq):
    n_tail = -(-(tq + KT - math.gcd(tq, KT)) // KT)
    return n_tail, (n_tail * KT - tq) // LANES


def _attn(z, lam_params, subln_col, lambda_init):
    b, l, _ = z.shape
    tq = 5 * LANES
    assert l % tq == 0 and tq % LANES == 0 and KT % LANES == 0
    n_pad = _attn_tail(tq)[1]
    qb, kb, vb = POOL_WIDTH // LANES, (POOL_WIDTH + DIFF_WIDTH) // LANES, (POOL_WIDTH + 2 * DIFF_WIDTH) // LANES
    row = pltpu.VMEM((1, tq), F32)
    acc = pltpu.VMEM((DIFF_V_DIM, tq), F32)
    sbuf = pltpu.VMEM((2, KT, tq), F32)
    tbuf = pltpu.VMEM((2, 1, tq), F32)
    return pl.pallas_call(
        functools.partial(_attn_kernel, tq=tq, lambda_init=lambda_init),
        grid=(b, DIFF_HEADS, l // tq),
        in_specs=[pl.BlockSpec((4, DIFF_QK_DIM), lambda bi, h, i: (0, 0)),
                  pl.BlockSpec((None, tq, LANES), lambda bi, h, i: (bi, i, qb + h)),
                  pl.BlockSpec((None, l, LANES), lambda bi, h, i: (bi, 0, kb + h)),
                  pl.BlockSpec((None, l, LANES), lambda bi, h, i: (bi, 0, vb + h)),
                  pl.BlockSpec((DIFF_V_DIM, 1), lambda bi, h, i: (0, 0))],
        out_specs=pl.BlockSpec((None, tq, DIFF_V_DIM), lambda bi, h, i: (bi, i, h)),
        out_shape=jax.ShapeDtypeStruct((b, l, DIFF_WIDTH), BF16),
        scratch_shapes=[pltpu.VMEM((l + n_pad * LANES, LANES), BF16), pltpu.VMEM((l + n_pad * LANES, LANES), BF16),
                        pltpu.VMEM((l // LANES + n_pad, DIFF_V_DIM + ONES_ROWS, LANES), BF16),
                        sbuf, sbuf, tbuf, tbuf, row, row, row, row, acc, acc],
        compiler_params=_cparams(("parallel", "parallel", "arbitrary")),
        name="diff_attn",
    )(lam_params, z, z, z, subln_col)


def _gla_kernel(q_ref, k_ref, v_ref, r_ref, glr_ref, w2_ref, gb_ref, hn_ref, o_ref, st_ref, *, c_len, sb):
    c = pl.program_id(0)
    n_batch = q_ref.shape[0]

    @pl.when(c == 0)
    def _():
        st_ref[...] = jnp.zeros(st_ref.shape, F32)

    pos = c * c_len + lax.broadcasted_iota(jnp.int32, (c_len, 1), 0)
    validf = (pos >= FIRST_VALID).astype(F32)
    ri = lax.broadcasted_iota(jnp.int32, (c_len, c_len), 0)
    ci = lax.broadcasted_iota(jnp.int32, (c_len, c_len), 1)
    tri = jnp.where(ri >= ci, 1.0, 0.0).astype(BF16)
    row = lax.broadcasted_iota(jnp.int32, (c_len, 1), 0)
    t_loc = lax.broadcasted_iota(jnp.int32, (sb, 1), 0)
    lane_c = lax.broadcasted_iota(jnp.int32, (sb, c_len), 1)
    nt = (((1,), (1,)), ((), ()))
    tn = (((0,), (0,)), ((), ()))

    for bi, h in [(bi, h) for bi in range(n_batch) for h in range(GLA_HEADS)]:
        glr = glr_ref[bi]
        ks = slice(h * GLA_DK, (h + 1) * GLA_DK)
        vs = slice(h * GLA_DV, (h + 1) * GLA_DV)
        g = jnp.dot(glr, w2_ref[:, ks], preferred_element_type=F32) + gb_ref[:, ks]
        log_a = -(jnp.maximum(-g, 0.0) + jnp.log(1.0 + jnp.exp(-jnp.abs(g)))) * (1.0 / GLA_TAU)
        la_hi = log_a.astype(BF16)
        la_lo = (log_a - la_hi.astype(F32)).astype(BF16)
        b = (jnp.dot(tri, la_hi, preferred_element_type=F32)
             + jnp.dot(tri, la_lo, preferred_element_type=F32))
        q = q_ref[bi, :, ks].astype(F32) * (GLA_DK ** -0.5)
        k = k_ref[bi, :, ks].astype(F32) * validf
        v = v_ref[bi, :, vs]
        st = st_ref[bi, h]

        b2 = b * LOG2E
        blocks = []
        for i in range(c_len // sb):
            lo = i * sb
            q_i = q[lo:lo + sb]
            b_i = b[lo:lo + sb]
            b2_i = b2[lo:lo + sb]
            a_i = jnp.zeros((sb, c_len), F32)
            if i > 0:
                b_ref_row = b[lo - 1:lo]
                q_t = q_i * jnp.exp(b_i - b_ref_row)
                k_t = jnp.where(row < lo, k * jnp.exp(jnp.minimum(b_ref_row - b, 0.0)), 0.0)
                a_i = lax.dot_general(q_t.astype(BF16), k_t.astype(BF16), nt, preferred_element_type=F32)
            for s in range(sb):
                r = lo + s
                e = jnp.exp2(b2_i - b2[r:r + 1])
                col = jnp.sum(q_i * k[r:r + 1] * e, axis=-1, keepdims=True)
                a_i = jnp.where(lane_c == r, col, a_i)
            blocks.append(jnp.where(lane_c <= lo + t_loc, a_i, 0.0))
        att = jnp.concatenate(blocks, axis=0)

        o = jnp.dot(att.astype(BF16), v, preferred_element_type=F32)
        o = o + lax.dot_general((q * jnp.exp(b)).astype(BF16), st.astype(BF16), nt,
                                preferred_element_type=F32)
        b_last = b[c_len - 1:c_len]
        k_hat = (k * jnp.exp(b_last - b)).astype(BF16)
        st_ref[bi, h] = st * jnp.exp(b_last) + lax.dot_general(v, k_hat, tn, preferred_element_type=F32)

        o = o * lax.rsqrt(jnp.mean(o * o, axis=-1, keepdims=True) + 1e-6) * hn_ref[...]
        rg = r_ref[bi, :, vs].astype(F32)
        o_ref[bi, :, vs] = (o * (rg * jax.nn.sigmoid(rg))).astype(o_ref.dtype)


def _gla(z, w2, gate_b, head_norm):
    b, l, _ = z.shape
    c_len = 128
    qw, vw = GLA_QK_WIDTH, GLA_V_WIDTH
    return pl.pallas_call(
        functools.partial(_gla_kernel, c_len=c_len, sb=16),
        grid=(l // c_len,),
        in_specs=[pl.BlockSpec((b, c_len, qw), lambda c: (0, c, 0)),
                  pl.BlockSpec((b, c_len, qw), lambda c: (0, c, 1)),
                  pl.BlockSpec((b, c_len, vw), lambda c: (0, c, 1)),
                  pl.BlockSpec((b, c_len, vw), lambda c: (0, c, 2)),
                  pl.BlockSpec((b, c_len, LANES), lambda c: (0, c, (2 * qw + 2 * vw) // LANES)),
                  pl.BlockSpec((LANES, qw), lambda c: (0, 0)),
                  pl.BlockSpec((1, qw), lambda c: (0, 0)),
                  pl.BlockSpec((1, GLA_DV), lambda c: (0, 0))],
        out_specs=pl.BlockSpec((b, c_len, vw), lambda c: (0, c, 0)),
        out_shape=jax.ShapeDtypeStruct((b, l, vw), BF16),
        scratch_shapes=[pltpu.VMEM((b, GLA_HEADS, GLA_DV, GLA_DK), F32)],
        compiler_params=_cparams(("arbitrary",)),
        name="gla_mixer",
    )(z, z, z, z, z, w2, gate_b, head_norm)


def _route_tokens(x, bias_ref, rw_ref, haug_ref, bucket_ref, rank_ref, cnt_ref, upper_scr, cnt_scr):
    tm, d = x.shape

    @pl.when(pl.program_id(0) == 0)
    def _():
        r = lax.broadcasted_iota(jnp.int32, (tm, tm), 0)
        c = lax.broadcasted_iota(jnp.int32, (tm, tm), 1)
        upper_scr[...] = jnp.where(r < c, 1.0, 0.0).astype(BF16)
        cnt_scr[...] = jnp.zeros(cnt_scr.shape, F32)

    w = rw_ref[...]
    xh = x.astype(BF16)
    xl = (x - xh.astype(F32)).astype(BF16)
    wh = w.astype(BF16)
    wl = (w - wh.astype(F32)).astype(BF16)
    nt = (((1,), (1,)), ((), ()))
    logits = (lax.dot_general(wh, xh, nt, preferred_element_type=F32)
              + lax.dot_general(wh, xl, nt, preferred_element_type=F32)
              + lax.dot_general(wl, xh, nt, preferred_element_type=F32))
    lg = [logits[e:e + 1, :] for e in range(N_EXPERTS)]
    mx = functools.reduce(jnp.maximum, lg)
    ex = [jnp.exp(v - mx) for v in lg]
    den = functools.reduce(jnp.add, ex)
    probs = [v / den for v in ex]
    sel = [probs[e] + bias_ref[e] for e in range(N_EXPERTS)]

    def top2_sum(a, b, c, d_):
        hi1, lo1 = jnp.maximum(a, b), jnp.minimum(a, b)
        hi2, lo2 = jnp.maximum(c, d_), jnp.minimum(c, d_)
        return jnp.maximum(hi1, hi2) + jnp.maximum(jnp.minimum(hi1, hi2), jnp.maximum(lo1, lo2))

    best = top2_sum(*sel[0:4])
    gidx = jnp.zeros((1, tm), jnp.int32)
    for g in range(1, N_GROUPS):
        sc = top2_sum(*sel[4 * g:4 * g + 4])
        better = sc > best
        gidx = jnp.where(better, g, gidx)
        best = jnp.maximum(best, sc)

    def pick(vals, j):
        out = vals[j]
        for g in range(1, N_GROUPS):
            out = jnp.where(gidx == g, vals[4 * g + j], out)
        return out

    sg = [pick(sel, j) for j in range(EXPERTS_PER_GROUP)]
    pg = [pick(probs, j) for j in range(EXPERTS_PER_GROUP)]
    picked = []
    for j in range(EXPERTS_PER_GROUP):
        rank = jnp.zeros((1, tm), jnp.int32)
        for o in range(EXPERTS_PER_GROUP):
            if o == j:
                continue
            ahead = (sg[o] > sg[j]) if o > j else (sg[o] >= sg[j])
            rank = rank + ahead.astype(jnp.int32)
        picked.append(rank < TOP_K)
    chosen = [jnp.where(picked[j], pg[j], 0.0) for j in range(EXPERTS_PER_GROUP)]
    wsum = functools.reduce(jnp.add, chosen)
    wn = [cj / wsum for cj in chosen]

    pair = jnp.where(picked[0],
                     jnp.where(picked[1], 0, jnp.where(picked[2], 1, 2)),
                     jnp.where(picked[1], jnp.where(picked[2], 3, 4), 5))
    w_lo = jnp.where(picked[0], wn[0], jnp.where(picked[1], wn[1], wn[2]))
    w_hi = jnp.where(picked[3], wn[3], jnp.where(picked[2], wn[2], wn[1]))
    bucket = gidx * len(PAIRS) + pair

    onehot = jnp.where(lax.broadcasted_iota(jnp.int32, (BUCKET_ROWS, tm), 0) == bucket, 1.0, 0.0)
    before = jnp.dot(onehot.astype(BF16), upper_scr[...], preferred_element_type=F32)
    seen = before + cnt_scr[:, 0:1]
    bucket_ref[...] = bucket
    rank_ref[...] = jnp.sum(onehot * seen, axis=0, keepdims=True).astype(jnp.int32)
    cnt_scr[...] = cnt_scr[...] + jnp.sum(onehot, axis=1, keepdims=True)
    cnt_ref[...] = cnt_scr[...]

    pay_t = jnp.concatenate([w_lo, w_hi, jnp.zeros((LANES - TOP_K, tm), F32)], axis=0)
    haug_ref[:, 0:d] = x
    haug_ref[:, d:] = pay_t.T


def _mixout_route_kernel(*refs):
    bias_ref, *y_refs, w_ref, h_ref, g_ref, b_ref, rw_ref = refs[:-6]
    acc = ALPHA * h_ref[...]
    off = 0
    for y_ref in y_refs:
        kd = y_ref.shape[1]
        acc = acc + jnp.dot(y_ref[...], w_ref[off:off + kd, :], preferred_element_type=F32)
        off += kd
    _route_tokens(_layer_norm(acc, g_ref[...], b_ref[...]), bias_ref, rw_ref, *refs[-6:])


def _mixout_route(ys, w, h2d, g, b, router_w_t, router_bias, name):
    n, d = h2d.shape
    tm = _pick(n, (640, 512, 256, 128))
    rows = lambda width: pl.BlockSpec((tm, width), lambda i: (i, 0))
    whole = lambda a: pl.BlockSpec(a.shape, lambda i: (0, 0))
    per_tile = pl.BlockSpec((None, 1, tm), lambda i: (i, 0, 0))
    return pl.pallas_call(
        _mixout_route_kernel,
        grid=(n // tm,),
        in_specs=[pl.BlockSpec(memory_space=pltpu.SMEM)] + [rows(y.shape[1]) for y in ys]
        + [whole(w), rows(d), whole(g), whole(b), whole(router_w_t)],
        out_specs=[rows(d + LANES), per_tile, per_tile, pl.BlockSpec((BUCKET_ROWS, LANES), lambda i: (0, 0))],
        out_shape=[jax.ShapeDtypeStruct((n, d + LANES), F32),
                   jax.ShapeDtypeStruct((n // tm, 1, tm), jnp.int32),
                   jax.ShapeDtypeStruct((n // tm, 1, tm), jnp.int32),
                   jax.ShapeDtypeStruct((BUCKET_ROWS, LANES), F32)],
        scratch_shapes=[pltpu.VMEM((tm, tm), BF16), pltpu.VMEM((BUCKET_ROWS, LANES), F32)],
        compiler_params=_cparams(("arbitrary",)),
        name=name,
    )(router_bias, *ys, w, h2d, g, b, router_w_t)


def _row_dma(src_ref, dst_ref, sem, src_row, dst_row):
    return pltpu.make_async_copy(src_ref.at[pl.ds(src_row, 1), :], dst_ref.at[pl.ds(dst_row, 1), :], sem)


def _scatter_rows_kernel(idx_ref, src_ref, dst_ref, sem, *, rows):
    base = pl.program_id(0) * rows
    for r in range(rows):
        _row_dma(src_ref, dst_ref, sem, r, idx_ref[base + r]).start(priority=r % 2)

    def drain(r, c):
        _row_dma(src_ref, dst_ref, sem, 0, 0).wait()
        return c

    lax.fori_loop(0, rows, drain, 0, unroll=8)


def _gather_rows_kernel(idx_ref, src_ref, dst_ref, sem, *, rows):
    base = pl.program_id(0) * rows
    for r in range(rows):
        _row_dma(src_ref, dst_ref, sem, idx_ref[base + r], r).start(priority=r % 2)

    def drain(r, c):
        _row_dma(src_ref, dst_ref, sem, 0, 0).wait()
        return c

    lax.fori_loop(0, rows, drain, 0, unroll=8)


def _permute_rows(body, idx, src, n_dst, name):
    m = idx.shape[0]
    width = src.shape[1]
    rows = _pick(m, (512, 256, 128))
    tile = pl.BlockSpec((rows, width), lambda i, idx_ref: (i, 0))
    whole = pl.BlockSpec(memory_space=pl.ANY)
    scatter = body is _scatter_rows_kernel
    return pl.pallas_call(
        functools.partial(body, rows=rows),
        grid_spec=pltpu.PrefetchScalarGridSpec(
            num_scalar_prefetch=1,
            grid=(m // rows,),
            in_specs=[tile if scatter else whole],
            out_specs=whole if scatter else tile,
            scratch_shapes=[pltpu.SemaphoreType.DMA(())]),
        out_shape=jax.ShapeDtypeStruct((n_dst, width), src.dtype),
        compiler_params=_cparams(("arbitrary",)),
        name=name,
    )(idx, src)


def _moe_kernel(blk_ref, e_lo_ref, e_hi_ref, new_lo_ref, new_hi_ref, nv_ref, x_ref,
                wg_lo, wg_hi, wu_lo, wu_hi, wd_lo, wd_hi, g_ref, b_ref, o_ref, wg_scr, wu_scr, wd_scr):
    t = pl.program_id(0)
    nv = nv_ref[t]

    for j, (flag_ref, wg, wu, wd) in enumerate(((new_lo_ref, wg_lo, wu_lo, wd_lo), (new_hi_ref, wg_hi, wu_hi, wd_hi))):
        @pl.when(flag_ref[t] > 0)
        def _():
            wg_scr[j] = wg[...].astype(BF16)
            wu_scr[j] = wu[...].astype(BF16)
            wd_scr[j] = wd[...].astype(BF16)

    @pl.when(nv > 0)
    def _():
        tm, d = o_ref.shape
        live = lax.broadcasted_iota(jnp.int32, (tm, 1), 0) < nv
        x = jnp.where(live, x_ref[:, 0:d], 0.0)
        cw = jnp.where(live, x_ref[:, d:], 0.0)
        xb = x.astype(BF16)
        acc = ALPHA * x
        for j in range(TOP_K):
            gate = jnp.dot(xb, wg_scr[j], preferred_element_type=F32)
            up = jnp.dot(xb, wu_scr[j], preferred_element_type=F32)
            act = gate * jax.nn.sigmoid(gate) * up * cw[:, j:j + 1]
            acc = acc + jnp.dot(act.astype(BF16), wd_scr[j], preferred_element_type=F32)
        o_ref[...] = _layer_norm(acc, g_ref[...], b_ref[...])


def _moe(x_sorted, tables, wg, wu, wd, layer, g, b, tm, name):
    rows, wa = x_sorted.shape
    d = wa - LANES
    n_tiles = tables[0].shape[0]
    rows_of = lambda t, blk, e_lo, e_hi, new_lo, new_hi, nv: (blk[t], 0)
    lo = lambda t, blk, e_lo, e_hi, new_lo, new_hi, nv: (layer, e_lo[t], 0, 0)
    hi = lambda t, blk, e_lo, e_hi, new_lo, new_hi, nv: (layer, e_hi[t], 0, 0)
    const = lambda t, blk, e_lo, e_hi, new_lo, new_hi, nv: (0, 0)
    w_in = lambda f: pl.BlockSpec((None, None, d, D_EXPERT), f)
    w_out = lambda f: pl.BlockSpec((None, None, D_EXPERT, d), f)
    return pl.pallas_call(
        _moe_kernel,
        grid_spec=pltpu.PrefetchScalarGridSpec(
            num_scalar_prefetch=6,
            grid=(n_tiles,),
            in_specs=[pl.BlockSpec((tm, wa), rows_of),
                      w_in(lo), w_in(hi), w_in(lo), w_in(hi), w_out(lo), w_out(hi),
                      pl.BlockSpec((1, d), const), pl.BlockSpec((1, d), const)],
            out_specs=pl.BlockSpec((tm, d), rows_of),
            scratch_shapes=[pltpu.VMEM((TOP_K, d, D_EXPERT), BF16), pltpu.VMEM((TOP_K, d, D_EXPERT), BF16),
                            pltpu.VMEM((TOP_K, D_EXPERT, d), BF16)]),
        out_shape=jax.ShapeDtypeStruct((rows, d), F32),
        compiler_params=_cparams(("arbitrary",)),
        name=name,
    )(*tables, x_sorted, wg, wg, wu, wu, wd, wd, g, b)


def _bucket_layout(counts, tm, n_tiles):
    tiles_b = (counts + tm - 1) // tm
    ends = jnp.cumsum(tiles_b)
    starts = ends - tiles_b
    t = jnp.arange(n_tiles, dtype=jnp.int32)
    tc = jnp.minimum(t, ends[-1] - 1)
    bkt = jnp.sum((tc[:, None] >= ends[None, :]).astype(jnp.int32), axis=1)
    nv = jnp.where(t < ends[-1], jnp.clip(counts[bkt] - (tc - starts[bkt]) * tm, 0, tm), 0)
    pair_lo = jnp.array([p[0] for p in PAIRS], jnp.int32)
    pair_hi = jnp.array([p[1] for p in PAIRS], jnp.int32)
    grp, pair = bkt // len(PAIRS), bkt % len(PAIRS)
    e_lo = grp * EXPERTS_PER_GROUP + pair_lo[pair]
    e_hi = grp * EXPERTS_PER_GROUP + pair_hi[pair]
    changed = lambda e: jnp.concatenate([jnp.ones((1,), jnp.int32), (e[1:] != e[:-1]).astype(jnp.int32)])
    i32 = lambda a: a.astype(jnp.int32)
    return i32(starts * tm), (i32(tc), i32(e_lo), i32(e_hi), changed(e_lo), changed(e_hi), i32(nv))


def _routed_moe(routed, wg, wu, wd, layer, g, b, out_idx_fn, n_out, name):
    haug, bucket, rank, cnt = routed
    n = haug.shape[0]
    tm = _pick(n, (256, 128))
    n_tiles = n // tm + N_BUCKETS
    counts = cnt[:N_BUCKETS, 0].astype(jnp.int32)
    row0, tables = _bucket_layout(counts, tm, n_tiles)
    pos = row0[bucket.reshape(n)] + rank.reshape(n)
    x_sorted = _permute_rows(_scatter_rows_kernel, pos, haug, n_tiles * tm, name + "_scatter")
    y_sorted = _moe(x_sorted, tables, wg, wu, wd, layer, g, b, tm, name)
    return _permute_rows(_gather_rows_kernel, out_idx_fn(pos), y_sorted, n_out, name + "_gather")


def kernel(x, meta, even_w_in, pool_w, pool_scale, diff_lq1, diff_lk1, diff_lq2, diff_lk2, diff_subln,
           even_w_out, odd_w_in, gla_gate_w2, gla_gate_b, gla_head_norm, odd_w_out, ln_mix_g, ln_mix_b,
           ln_ffn_g, ln_ffn_b, router_w, router_bias, moe_w_gate, moe_w_up, moe_w_down):
    bsz, seq, d = x.shape
    l = PREFIX + seq
    n = bsz * l
    pad = jnp.zeros((bsz, FIRST_VALID, d), x.dtype)
    metas = jnp.broadcast_to(meta.astype(x.dtype)[None], (bsz, N_META, d))
    h = jnp.concatenate([pad, metas, x], axis=1).reshape(n, d)
    router_w_t = router_w.T
    row = lambda a: a.reshape(1, -1)

    for i in range(DEPTH):
        j = i // 2
        if i % 2 == 0:
            lambda_init = 0.8 - 0.6 * math.exp(-0.3 * i)
            z = _proj(h, even_w_in[j].astype(BF16), "even_in_proj").reshape(bsz, l, EVEN_IN)
            y_pool = _pool(z, pool_w[j].astype(BF16), row(pool_scale[j]))
            lam_params = jnp.stack([diff_lq1[j], diff_lk1[j], diff_lq2[j], diff_lk2[j]])
            y_attn = _attn(z, lam_params, diff_subln[j].reshape(-1, 1), lambda_init)
            ys = [y_pool.reshape(n, POOL_WIDTH), y_attn.reshape(n, DIFF_WIDTH)]
            routed = _mixout_route(ys, even_w_out[j].astype(BF16), h, row(ln_mix_g[i]), row(ln_mix_b[i]),
                                   router_w_t, router_bias, "even_out_proj")
        else:
            w_in = jnp.pad(odd_w_in[j], ((0, 0), (0, ODD_IN_PAD - ODD_IN))).astype(BF16)
            z = _proj(h, w_in, "odd_in_proj").reshape(bsz, l, ODD_IN_PAD)
            w2 = jnp.pad(gla_gate_w2[j], ((0, LANES - GLA_RANK), (0, 0))).astype(BF16)
            y = _gla(z, w2, row(gla_gate_b[j]), row(gla_head_norm[j]))
            routed = _mixout_route([y.reshape(n, GLA_V_WIDTH)], odd_w_out[j].astype(BF16), h,
                                   row(ln_mix_g[i]), row(ln_mix_b[i]), router_w_t, router_bias, "odd_out_proj")
        last = i == DEPTH - 1
        if last:
            out_idx_fn = lambda pos: pos.reshape(bsz, l)[:, PREFIX:].reshape(-1)
        else:
            out_idx_fn = lambda pos: pos
        h = _routed_moe(routed, moe_w_gate, moe_w_up, moe_w_down, i, row(ln_ffn_g[i]), row(ln_ffn_b[i]),
                        out_idx_fn, bsz * seq if last else n, f"moe_{i}")
    return h.reshape(bsz, seq, d)
```

```python
import functools
import math

import jax
import jax.numpy as jnp
from jax import lax
from jax.experimental import pallas as pl
from jax.experimental.pallas import tpu as pltpu

F32 = jnp.float32
BF16 = jnp.bfloat16

D_MODEL = 1024
DEPTH = 2
N_META = 16
PREFIX = 128
FIRST_VALID = PREFIX - N_META
POOL_WINDOWS = (2, 4, 8, 16)
POOL_GROUP = 128
POOL_WIDTH = 512
MAX_WINDOW = 16
DIFF_HEADS = 4
DIFF_QK_DIM = 64
DIFF_V_DIM = 128
DIFF_WIDTH = 512
EVEN_IN = 2048
GLA_HEADS = 4
GLA_DK = 128
GLA_DV = 256
GLA_RANK = 16
GLA_TAU = 16.0
GLA_QK_WIDTH = GLA_HEADS * GLA_DK
GLA_V_WIDTH = GLA_HEADS * GLA_DV
ODD_IN = 2 * GLA_QK_WIDTH + 2 * GLA_V_WIDTH + GLA_RANK
ODD_IN_PAD = 3200
N_EXPERTS = 16
N_GROUPS = 4
EXPERTS_PER_GROUP = 4
D_EXPERT = 512
TOP_K = 2
PAIRS = ((0, 1), (0, 2), (0, 3), (1, 2), (1, 3), (2, 3))
N_BUCKETS = N_GROUPS * len(PAIRS)
BUCKET_ROWS = 32
ALPHA = (2.0 * DEPTH) ** 0.25
NEG_BIG = -1e30
LOG2E = math.log2(math.e)
KT = 640
ONES_ROWS = 16

LANES = 128
VMEM_LIMIT = 56 * 1024 * 1024


def _pick(n, candidates):
    for c in candidates:
        if n % c == 0:
            return c
    raise ValueError(f"no tile for {n} in {candidates}")


def _cparams(sem):
    return pltpu.CompilerParams(dimension_semantics=sem, vmem_limit_bytes=VMEM_LIMIT)


def _layer_norm(a, g, b):
    mu = jnp.mean(a, axis=-1, keepdims=True)
    d = a - mu
    var = jnp.mean(d * d, axis=-1, keepdims=True)
    return d * lax.rsqrt(var + 1e-5) * g + b


def _proj_kernel(x_ref, w_ref, o_ref, *, tn):
    xb = x_ref[...].astype(BF16)
    for j in range(o_ref.shape[1] // tn):
        cols = slice(j * tn, (j + 1) * tn)
        o_ref[:, cols] = jnp.dot(xb, w_ref[:, cols], preferred_element_type=F32).astype(o_ref.dtype)


def _stream_tile(main_ref, prev_ref, prefix_ref, t):
    head = jnp.where(t == 0, prefix_ref[...], prev_ref[...])
    return jnp.concatenate([head, main_ref[0:main_ref.shape[0] - PREFIX, :]], axis=0)


def _stream_specs(seq, d, tm):
    tiles = (PREFIX + seq) // tm
    per = tm // PREFIX
    main = pl.BlockSpec((None, tm, d), lambda i: (i // tiles, i % tiles, 0))
    prev = pl.BlockSpec((None, PREFIX, d), lambda i: (i // tiles, jnp.maximum((i % tiles) * per - 1, 0), 0))
    prefix = pl.BlockSpec((PREFIX, d), lambda i: (0, 0))
    return tiles, [main, prev, prefix]


def _proj_stream_kernel(main_ref, prev_ref, prefix_ref, w_ref, o_ref, *, tn, tiles):
    xb = _stream_tile(main_ref, prev_ref, prefix_ref, pl.program_id(0) % tiles).astype(BF16)
    for j in range(o_ref.shape[1] // tn):
        cols = slice(j * tn, (j + 1) * tn)
        o_ref[:, cols] = jnp.dot(xb, w_ref[:, cols], preferred_element_type=F32).astype(o_ref.dtype)


def _proj_stream(x, prefix, w, tm, name):
    bsz, seq, k = x.shape
    wout = w.shape[1]
    tn = _pick(wout, (640, 512, 128))
    tiles, stream = _stream_specs(seq, k, tm)
    return pl.pallas_call(
        functools.partial(_proj_stream_kernel, tn=tn, tiles=tiles),
        grid=(bsz * tiles,),
        in_specs=stream + [pl.BlockSpec((k, wout), lambda i: (0, 0))],
        out_specs=pl.BlockSpec((tm, wout), lambda i: (i, 0)),
        out_shape=jax.ShapeDtypeStruct((bsz * tiles * tm, wout), BF16),
        compiler_params=_cparams(("parallel",)),
        name=name,
    )(x, x, prefix, w)


def _proj(x2d, w, name):
    n, k = x2d.shape
    wout = w.shape[1]
    tm = _pick(n, (640, 512, 256, 128))
    tn = _pick(wout, (640, 512, 128))
    return pl.pallas_call(
        functools.partial(_proj_kernel, tn=tn),
        grid=(n // tm,),
        in_specs=[pl.BlockSpec((tm, k), lambda i: (i, 0)),
                  pl.BlockSpec((k, wout), lambda i: (0, 0))],
        out_specs=pl.BlockSpec((tm, wout), lambda i: (i, 0)),
        out_shape=jax.ShapeDtypeStruct((n, wout), BF16),
        compiler_params=_cparams(("parallel",)),
        name=name,
    )(x2d, w)


def _pool_kernel(cur_ref, halo_ref, w_ref, sc_ref, o_ref, u_scr):
    t = pl.program_id(1)
    tl = cur_ref.shape[0]
    pos = t * tl + lax.broadcasted_iota(jnp.int32, (tl, 1), 0)
    hpos = t * tl - MAX_WINDOW + lax.broadcasted_iota(jnp.int32, (MAX_WINDOW, 1), 0)
    u_scr[0:MAX_WINDOW, :] = jnp.where(hpos >= FIRST_VALID, halo_ref[...].astype(F32), 0.0)
    u_scr[MAX_WINDOW:, :] = jnp.where(pos >= FIRST_VALID, cur_ref[...].astype(F32), 0.0)
    n_valid = pos - (FIRST_VALID - 1)
    for gi, w in enumerate(POOL_WINDOWS):
        cols = slice(gi * POOL_GROUP, (gi + 1) * POOL_GROUP)
        u = u_scr[MAX_WINDOW:MAX_WINDOW + tl, cols]
        win = u
        for j in range(1, w):
            win = win + u_scr[MAX_WINDOW - j:MAX_WINDOW - j + tl, cols]
        cnt = jnp.clip(n_valid, 1, w).astype(F32)
        d = win / cnt - u
        y = jnp.dot(d.astype(BF16), w_ref[gi], preferred_element_type=F32) * sc_ref[:, cols]
        o_ref[:, cols] = y.astype(o_ref.dtype)


def _pool(z, pool_w, pool_scale):
    b, l, _ = z.shape
    tl = _pick(l, (640, 128))
    hb = tl // MAX_WINDOW
    return pl.pallas_call(
        _pool_kernel,
        grid=(b, l // tl),
        in_specs=[pl.BlockSpec((None, tl, POOL_WIDTH), lambda bi, t: (bi, t, 0)),
                  pl.BlockSpec((None, MAX_WINDOW, POOL_WIDTH),
                               lambda bi, t: (bi, jnp.maximum(t * hb - 1, 0), 0)),
                  pl.BlockSpec((len(POOL_WINDOWS), POOL_GROUP, POOL_GROUP), lambda bi, t: (0, 0, 0)),
                  pl.BlockSpec((1, POOL_WIDTH), lambda bi, t: (0, 0))],
        out_specs=pl.BlockSpec((None, tl, POOL_WIDTH), lambda bi, t: (bi, t, 0)),
        out_shape=jax.ShapeDtypeStruct((b, l, POOL_WIDTH), BF16),
        scratch_shapes=[pltpu.VMEM((tl + MAX_WINDOW, POOL_WIDTH), F32)],
        compiler_params=_cparams(("parallel", "parallel")),
        name="pool_mixer",
    )(z, z, pool_w, pool_scale)


def _attn_kernel(lam_ref, q_ref, k_ref, v_ref, sub_ref, o_ref, ka_scr, kb_scr, vt_scr,
                 s0_scr, s1_scr, t0_scr, t1_scr, m0_scr, m1_scr, l0_scr, l1_scr, acc0_scr, acc1_scr,
                 *, tq, lambda_init):
    h = pl.program_id(1)
    i = pl.program_id(2)
    slope = jnp.exp2(-2.0 * (h + 1).astype(F32))
    half = DIFF_QK_DIM
    n_chunks = k_ref.shape[0] // LANES
    n_tail, n_pad = _attn_tail(tq)

    @pl.when(i == 0)
    def _():
        lane = lax.broadcasted_iota(jnp.int32, (LANES, 2 * half), 1)
        rowf = lax.broadcasted_iota(jnp.int32, (LANES, 2 * half), 0).astype(F32)

        def build(c, carry):
            rows = pl.ds(pl.multiple_of(c * LANES, LANES), LANES)
            kt = k_ref[rows, :].astype(F32)
            hi = jnp.full((LANES, 2 * half), c, jnp.int32).astype(F32)
            ka = jnp.where(lane < half, kt, jnp.where(lane == half, hi, jnp.where(lane == half + 1, rowf, 0.0)))
            kb = jnp.where(lane >= half, kt, jnp.where(lane == 0, hi, jnp.where(lane == 1, rowf, 0.0)))
            ka_scr[rows, :] = ka.astype(BF16)
            kb_scr[rows, :] = kb.astype(BF16)
            vt_scr[c, 0:DIFF_V_DIM, :] = v_ref[rows, :].astype(F32).T.astype(BF16)
            vt_scr[c, DIFF_V_DIM:, :] = jnp.ones((ONES_ROWS, LANES), BF16)
            return carry

        lax.fori_loop(0, n_chunks, build, 0)
        for extra in range(n_pad):
            pad_rows = pl.ds((n_chunks + extra) * LANES, LANES)
            ka_scr[pad_rows, :] = jnp.zeros((LANES, 2 * half), BF16)
            kb_scr[pad_rows, :] = jnp.zeros((LANES, 2 * half), BF16)
            vt_scr[n_chunks + extra] = jnp.zeros((DIFF_V_DIM + ONES_ROWS, LANES), BF16)

    lane = lax.broadcasted_iota(jnp.int32, (tq, 2 * half), 1)
    q = (q_ref[...] * jnp.asarray(half ** -0.5, BF16)).astype(F32)
    f_hi = LANES * slope
    qa = jnp.where(lane < half, q, jnp.where(lane == half, f_hi, jnp.where(lane == half + 1, slope, 0.0)))
    qb = jnp.where(lane >= half, q, jnp.where(lane == 0, f_hi, jnp.where(lane == 1, slope, 0.0)))
    q_maps = (qa.T.astype(BF16), qb.T.astype(BF16))
    k_maps = (ka_scr, kb_scr)
    s_scrs, t_scrs = (s0_scr, s1_scr), (t0_scr, t1_scr)
    m_scrs, l_scrs, acc_scrs = (m0_scr, m1_scr), (l0_scr, l1_scr), (acc0_scr, acc1_scr)
    for mp in range(2):
        m_scrs[mp][...] = jnp.full(m0_scr.shape, NEG_BIG, F32)
        l_scrs[mp][...] = jnp.zeros(l0_scr.shape, F32)
        acc_scrs[mp][...] = jnp.zeros(acc0_scr.shape, F32)
    q0 = i * tq
    n_full = q0 // KT

    def scores(t, mode):
        k0 = pl.multiple_of(t * KT, KT)
        buf = t % 2
        allowed = None
        if mode is not None:
            kpos = k0 + lax.broadcasted_iota(jnp.int32, (KT, tq), 0)
            allowed = kpos >= FIRST_VALID
            if mode == "diag":
                qpos = q0 + lax.broadcasted_iota(jnp.int32, (KT, tq), 1)
                allowed = jnp.logical_and(kpos <= qpos, jnp.logical_or(allowed, kpos == qpos))
        for mp in range(2):
            s = jnp.dot(k_maps[mp][pl.ds(k0, KT), :], q_maps[mp], preferred_element_type=F32)
            if allowed is not None:
                s = jnp.where(allowed, s, NEG_BIG)
            s_scrs[mp][buf] = s
            t_scrs[mp][buf] = jnp.max(s, axis=0, keepdims=True)

    def accumulate(t):
        c0 = t * (KT // LANES)
        buf = t % 2
        vt = jnp.concatenate([vt_scr[c0 + u] for u in range(KT // LANES)], axis=1)
        for mp in range(2):
            m_prev = m_scrs[mp][...]
            m_new = jnp.maximum(m_prev, t_scrs[mp][buf])
            alpha = jnp.exp(m_prev - m_new)
            p = jnp.exp(s_scrs[mp][buf] - m_new)
            pv = jnp.dot(vt, p.astype(BF16), preferred_element_type=F32)
            l_scrs[mp][...] = alpha * l_scrs[mp][...] + pv[DIFF_V_DIM:DIFF_V_DIM + 1, :]
            acc_scrs[mp][...] = alpha * acc_scrs[mp][...] + pv[0:DIFF_V_DIM, :]
            m_scrs[mp][...] = m_new

    @pl.when(n_full >= 1)
    def _():
        scores(0, "valid")

    @pl.when(n_full == 0)
    def _():
        scores(0, "diag")

    def steady(t, c):
        accumulate(t)
        scores(t + 1, None)
        return c

    lax.fori_loop(0, n_full - 1, steady, 0)

    @pl.when(n_full >= 1)
    def _():
        accumulate(n_full - 1)
        scores(n_full, "diag")

    for u in range(n_tail - 1):
        accumulate(n_full + u)
        scores(n_full + u + 1, "diag")
    accumulate(n_full + n_tail - 1)

    lam = (jnp.exp(jnp.sum(lam_ref[0:1, :] * lam_ref[1:2, :], axis=-1, keepdims=True))
           - jnp.exp(jnp.sum(lam_ref[2:3, :] * lam_ref[3:4, :], axis=-1, keepdims=True)) + lambda_init)
    o = acc0_scr[...] / l0_scr[...] - lam * (acc1_scr[...] / l1_scr[...])
    o = o * lax.rsqrt(jnp.mean(o * o, axis=0, keepdims=True) + 1e-6) * sub_ref[...] * (1.0 - lambda_init)
    o_ref[...] = o.T.astype(o_ref.dtype)


def _attn_tail(tq):
    n_tail = -(-(tq + KT - math.gcd(tq, KT)) // KT)
    return n_tail, (n_tail * KT - tq) // LANES


def _attn(z, lam_params, subln_col, lambda_init):
    b, l, _ = z.shape
    tq = 5 * LANES
    assert l % tq == 0 and tq % LANES == 0 and KT % LANES == 0
    n_pad = _attn_tail(tq)[1]
    qb, kb, vb = POOL_WIDTH // LANES, (POOL_WIDTH + DIFF_WIDTH) // LANES, (POOL_WIDTH + 2 * DIFF_WIDTH) // LANES
    row = pltpu.VMEM((1, tq), F32)
    acc = pltpu.VMEM((DIFF_V_DIM, tq), F32)
    sbuf = pltpu.VMEM((2, KT, tq), F32)
    tbuf = pltpu.VMEM((2, 1, tq), F32)
    return pl.pallas_call(
        functools.partial(_attn_kernel, tq=tq, lambda_init=lambda_init),
        grid=(b, DIFF_HEADS, l // tq),
        in_specs=[pl.BlockSpec((4, DIFF_QK_DIM), lambda bi, h, i: (0, 0)),
                  pl.BlockSpec((None, tq, LANES), lambda bi, h, i: (bi, i, qb + h)),
                  pl.BlockSpec((None, l, LANES), lambda bi, h, i: (bi, 0, kb + h)),
                  pl.BlockSpec((None, l, LANES), lambda bi, h, i: (bi, 0, vb + h)),
                  pl.BlockSpec((DIFF_V_DIM, 1), lambda bi, h, i: (0, 0))],
        out_specs=pl.BlockSpec((None, tq, DIFF_V_DIM), lambda bi, h, i: (bi, i, h)),
        out_shape=jax.ShapeDtypeStruct((b, l, DIFF_WIDTH), BF16),
        scratch_shapes=[pltpu.VMEM((l + n_pad * LANES, LANES), BF16), pltpu.VMEM((l + n_pad * LANES, LANES), BF16),
                        pltpu.VMEM((l // LANES + n_pad, DIFF_V_DIM + ONES_ROWS, LANES), BF16),
                        sbuf, sbuf, tbuf, tbuf, row, row, row, row, acc, acc],
        compiler_params=_cparams(("parallel", "parallel", "arbitrary")),
        name="diff_attn",
    )(lam_params, z, z, z, subln_col)


def _gla_kernel(q_ref, k_ref, v_ref, r_ref, glr_ref, w2_ref, gb_ref, hn_ref, o_ref, st_ref, *, c_len, sb):
    c = pl.program_id(0)
    n_batch = q_ref.shape[0]

    @pl.when(c == 0)
    def _():
        st_ref[...] = jnp.zeros(st_ref.shape, F32)

    pos = c * c_len + lax.broadcasted_iota(jnp.int32, (c_len, 1), 0)
    validf = (pos >= FIRST_VALID).astype(F32)
    ri = lax.broadcasted_iota(jnp.int32, (c_len, c_len), 0)
    ci = lax.broadcasted_iota(jnp.int32, (c_len, c_len), 1)
    tri = jnp.where(ri >= ci, 1.0, 0.0).astype(BF16)
    row = lax.broadcasted_iota(jnp.int32, (c_len, 1), 0)
    t_loc = lax.broadcasted_iota(jnp.int32, (sb, 1), 0)
    lane_c = lax.broadcasted_iota(jnp.int32, (sb, c_len), 1)
    nt = (((1,), (1,)), ((), ()))
    tn = (((0,), (0,)), ((), ()))

    for bi, h in [(bi, h) for bi in range(n_batch) for h in range(GLA_HEADS)]:
        glr = glr_ref[bi]
        ks = slice(h * GLA_DK, (h + 1) * GLA_DK)
        vs = slice(h * GLA_DV, (h + 1) * GLA_DV)
        g = jnp.dot(glr, w2_ref[:, ks], preferred_element_type=F32) + gb_ref[:, ks]
        log_a = -(jnp.maximum(-g, 0.0) + jnp.log(1.0 + jnp.exp(-jnp.abs(g)))) * (1.0 / GLA_TAU)
        la_hi = log_a.astype(BF16)
        la_lo = (log_a - la_hi.astype(F32)).astype(BF16)
        b = (jnp.dot(tri, la_hi, preferred_element_type=F32)
             + jnp.dot(tri, la_lo, preferred_element_type=F32))
        q = q_ref[bi, :, ks].astype(F32) * (GLA_DK ** -0.5)
        k = k_ref[bi, :, ks].astype(F32) * validf
        v = v_ref[bi, :, vs]
        st = st_ref[bi, h]

        b2 = b * LOG2E
        blocks = []
        for i in range(c_len // sb):
            lo = i * sb
            q_i = q[lo:lo + sb]
            b_i = b[lo:lo + sb]
            b2_i = b2[lo:lo + sb]
            a_i = jnp.zeros((sb, c_len), F32)
            if i > 0:
                b_ref_row = b[lo - 1:lo]
                q_t = q_i * jnp.exp(b_i - b_ref_row)
                k_t = jnp.where(row < lo, k * jnp.exp(jnp.minimum(b_ref_row - b, 0.0)), 0.0)
                a_i = lax.dot_general(q_t.astype(BF16), k_t.astype(BF16), nt, preferred_element_type=F32)
            for s in range(sb):
                r = lo + s
                e = jnp.exp2(b2_i - b2[r:r + 1])
                col = jnp.sum(q_i * k[r:r + 1] * e, axis=-1, keepdims=True)
                a_i = jnp.where(lane_c == r, col, a_i)
            blocks.append(jnp.where(lane_c <= lo + t_loc, a_i, 0.0))
        att = jnp.concatenate(blocks, axis=0)

        o = jnp.dot(att.astype(BF16), v, preferred_element_type=F32)
        o = o + lax.dot_general((q * jnp.exp(b)).astype(BF16), st.astype(BF16), nt,
                                preferred_element_type=F32)
        b_last = b[c_len - 1:c_len]
        k_hat = (k * jnp.exp(b_last - b)).astype(BF16)
        st_ref[bi, h] = st * jnp.exp(b_last) + lax.dot_general(v, k_hat, tn, preferred_element_type=F32)

        o = o * lax.rsqrt(jnp.mean(o * o, axis=-1, keepdims=True) + 1e-6) * hn_ref[...]
        rg = r_ref[bi, :, vs].astype(F32)
        o_ref[bi, :, vs] = (o * (rg * jax.nn.sigmoid(rg))).astype(o_ref.dtype)


def _gla(z, w2, gate_b, head_norm):
    b, l, _ = z.shape
    c_len = 128
    qw, vw = GLA_QK_WIDTH, GLA_V_WIDTH
    return pl.pallas_call(
        functools.partial(_gla_kernel, c_len=c_len, sb=16),
        grid=(l // c_len,),
        in_specs=[pl.BlockSpec((b, c_len, qw), lambda c: (0, c, 0)),
                  pl.BlockSpec((b, c_len, qw), lambda c: (0, c, 1)),
                  pl.BlockSpec((b, c_len, vw), lambda c: (0, c, 1)),
                  pl.BlockSpec((b, c_len, vw), lambda c: (0, c, 2)),
                  pl.BlockSpec((b, c_len, LANES), lambda c: (0, c, (2 * qw + 2 * vw) // LANES)),
                  pl.BlockSpec((LANES, qw), lambda c: (0, 0)),
                  pl.BlockSpec((1, qw), lambda c: (0, 0)),
                  pl.BlockSpec((1, GLA_DV), lambda c: (0, 0))],
        out_specs=pl.BlockSpec((b, c_len, vw), lambda c: (0, c, 0)),
        out_shape=jax.ShapeDtypeStruct((b, l, vw), BF16),
        scratch_shapes=[pltpu.VMEM((b, GLA_HEADS, GLA_DV, GLA_DK), F32)],
        compiler_params=_cparams(("arbitrary",)),
        name="gla_mixer",
    )(z, z, z, z, z, w2, gate_b, head_norm)


def _route_tokens(x, bias_ref, rw_ref, haug_ref, bucket_ref, rank_ref, cnt_ref, upper_scr, cnt_scr):
    tm, d = x.shape

    @pl.when(pl.program_id(0) == 0)
    def _():
        r = lax.broadcasted_iota(jnp.int32, (tm, tm), 0)
        c = lax.broadcasted_iota(jnp.int32, (tm, tm), 1)
        upper_scr[...] = jnp.where(r < c, 1.0, 0.0).astype(BF16)
        cnt_scr[...] = jnp.zeros(cnt_scr.shape, F32)

    w = rw_ref[...]
    xh = x.astype(BF16)
    xl = (x - xh.astype(F32)).astype(BF16)
    wh = w.astype(BF16)
    wl = (w - wh.astype(F32)).astype(BF16)
    nt = (((1,), (1,)), ((), ()))
    logits = (lax.dot_general(wh, xh, nt, preferred_element_type=F32)
              + lax.dot_general(wh, xl, nt, preferred_element_type=F32)
              + lax.dot_general(wl, xh, nt, preferred_element_type=F32))
    lg = [logits[e:e + 1, :] for e in range(N_EXPERTS)]
    mx = functools.reduce(jnp.maximum, lg)
    ex = [jnp.exp(v - mx) for v in lg]
    den = functools.reduce(jnp.add, ex)
    probs = [v / den for v in ex]
    sel = [probs[e] + bias_ref[e] for e in range(N_EXPERTS)]

    def top2_sum(a, b, c, d_):
        hi1, lo1 = jnp.maximum(a, b), jnp.minimum(a, b)
        hi2, lo2 = jnp.maximum(c, d_), jnp.minimum(c, d_)
        return jnp.maximum(hi1, hi2) + jnp.maximum(jnp.minimum(hi1, hi2), jnp.maximum(lo1, lo2))

    best = top2_sum(*sel[0:4])
    gidx = jnp.zeros((1, tm), jnp.int32)
    for g in range(1, N_GROUPS):
        sc = top2_sum(*sel[4 * g:4 * g + 4])
        better = sc > best
        gidx = jnp.where(better, g, gidx)
        best = jnp.maximum(best, sc)

    def pick(vals, j):
        out = vals[j]
        for g in range(1, N_GROUPS):
            out = jnp.where(gidx == g, vals[4 * g + j], out)
        return out

    sg = [pick(sel, j) for j in range(EXPERTS_PER_GROUP)]
    pg = [pick(probs, j) for j in range(EXPERTS_PER_GROUP)]
    picked = []
    for j in range(EXPERTS_PER_GROUP):
        rank = jnp.zeros((1, tm), jnp.int32)
        for o in range(EXPERTS_PER_GROUP):
            if o == j:
                continue
            ahead = (sg[o] > sg[j]) if o > j else (sg[o] >= sg[j])
            rank = rank + ahead.astype(jnp.int32)
        picked.append(rank < TOP_K)
    chosen = [jnp.where(picked[j], pg[j], 0.0) for j in range(EXPERTS_PER_GROUP)]
    wsum = functools.reduce(jnp.add, chosen)
    wn = [cj / wsum for cj in chosen]

    pair = jnp.where(picked[0],
                     jnp.where(picked[1], 0, jnp.where(picked[2], 1, 2)),
                     jnp.where(picked[1], jnp.where(picked[2], 3, 4), 5))
    w_lo = jnp.where(picked[0], wn[0], jnp.where(picked[1], wn[1], wn[2]))
    w_hi = jnp.where(picked[3], wn[3], jnp.where(picked[2], wn[2], wn[1]))
    bucket = gidx * len(PAIRS) + pair

    onehot = jnp.where(lax.broadcasted_iota(jnp.int32, (BUCKET_ROWS, tm), 0) == bucket, 1.0, 0.0)
    before = jnp.dot(onehot.astype(BF16), upper_scr[...], preferred_element_type=F32)
    seen = before + cnt_scr[:, 0:1]
    bucket_ref[...] = bucket
    rank_ref[...] = jnp.sum(onehot * seen, axis=0, keepdims=True).astype(jnp.int32)
    cnt_scr[...] = cnt_scr[...] + jnp.sum(onehot, axis=1, keepdims=True)
    cnt_ref[...] = cnt_scr[...]

    pay_t = jnp.concatenate([w_lo, w_hi, jnp.zeros((LANES - TOP_K, tm), F32)], axis=0)
    haug_ref[:, 0:d] = x
    haug_ref[:, d:] = pay_t.T


def _mixout_route_kernel(*refs, stream_tiles):
    if stream_tiles:
        bias_ref, *y_refs, w_ref, main_ref, prev_ref, prefix_ref, g_ref, b_ref, rw_ref = refs[:-6]
        resid = _stream_tile(main_ref, prev_ref, prefix_ref, pl.program_id(0) % stream_tiles)
    else:
        bias_ref, *y_refs, w_ref, h_ref, g_ref, b_ref, rw_ref = refs[:-6]
        resid = h_ref[...]
    acc = ALPHA * resid
    off = 0
    for y_ref in y_refs:
        kd = y_ref.shape[1]
        acc = acc + jnp.dot(y_ref[...], w_ref[off:off + kd, :], preferred_element_type=F32)
        off += kd
    _route_tokens(_layer_norm(acc, g_ref[...], b_ref[...]), bias_ref, rw_ref, *refs[-6:])


def _mixout_route(ys, w, resid, g, b, router_w_t, router_bias, tm, name):
    n = ys[0].shape[0]
    d = w.shape[1]
    rows = lambda width: pl.BlockSpec((tm, width), lambda i: (i, 0))
    whole = lambda a: pl.BlockSpec(a.shape, lambda i: (0, 0))
    per_tile = pl.BlockSpec((None, 1, tm), lambda i: (i, 0, 0))
    if isinstance(resid, tuple):
        x, prefix = resid
        stream_tiles, resid_specs = _stream_specs(x.shape[1], d, tm)
        resid_args = (x, x, prefix)
    else:
        stream_tiles, resid_specs, resid_args = 0, [rows(d)], (resid,)
    return pl.pallas_call(
        functools.partial(_mixout_route_kernel, stream_tiles=stream_tiles),
        grid=(n // tm,),
        in_specs=[pl.BlockSpec(memory_space=pltpu.SMEM)] + [rows(y.shape[1]) for y in ys]
        + [whole(w)] + resid_specs + [whole(g), whole(b), whole(router_w_t)],
        out_specs=[rows(d + LANES), per_tile, per_tile, pl.BlockSpec((BUCKET_ROWS, LANES), lambda i: (0, 0))],
        out_shape=[jax.ShapeDtypeStruct((n, d + LANES), F32),
                   jax.ShapeDtypeStruct((n // tm, 1, tm), jnp.int32),
                   jax.ShapeDtypeStruct((n // tm, 1, tm), jnp.int32),
                   jax.ShapeDtypeStruct((BUCKET_ROWS, LANES), F32)],
        scratch_shapes=[pltpu.VMEM((tm, tm), BF16), pltpu.VMEM((BUCKET_ROWS, LANES), F32)],
        compiler_params=_cparams(("arbitrary",)),
        name=name,
    )(router_bias, *ys, w, *resid_args, g, b, router_w_t)


def _row_dma(src_ref, dst_ref, sem, src_row, dst_row):
    return pltpu.make_async_copy(src_ref.at[pl.ds(src_row, 1), :], dst_ref.at[pl.ds(dst_row, 1), :], sem)


def _scatter_rows_kernel(idx_ref, src_ref, dst_ref, sem, *, rows):
    base = pl.program_id(0) * rows
    for r in range(rows):
        _row_dma(src_ref, dst_ref, sem, r, idx_ref[base + r]).start(priority=r % 2)

    def drain(r, c):
        _row_dma(src_ref, dst_ref, sem, 0, 0).wait()
        return c

    lax.fori_loop(0, rows, drain, 0, unroll=8)


def _gather_rows_kernel(idx_ref, src_ref, dst_ref, sem, *, rows):
    base = pl.program_id(0) * rows
    for r in range(rows):
        _row_dma(src_ref, dst_ref, sem, idx_ref[base + r], r).start(priority=r % 2)

    def drain(r, c):
        _row_dma(src_ref, dst_ref, sem, 0, 0).wait()
        return c

    lax.fori_loop(0, rows, drain, 0, unroll=8)


def _permute_rows(body, idx, src, n_dst, name):
    m = idx.shape[0]
    width = src.shape[1]
    rows = _pick(m, (512, 256, 128))
    tile = pl.BlockSpec((rows, width), lambda i, idx_ref: (i, 0))
    whole = pl.BlockSpec(memory_space=pl.ANY)
    scatter = body is _scatter_rows_kernel
    return pl.pallas_call(
        functools.partial(body, rows=rows),
        grid_spec=pltpu.PrefetchScalarGridSpec(
            num_scalar_prefetch=1,
            grid=(m // rows,),
            in_specs=[tile if scatter else whole],
            out_specs=whole if scatter else tile,
            scratch_shapes=[pltpu.SemaphoreType.DMA(())]),
        out_shape=jax.ShapeDtypeStruct((n_dst, width), src.dtype),
        compiler_params=_cparams(("arbitrary",)),
        name=name,
    )(idx, src)


def _moe_kernel(blk_ref, e_lo_ref, e_hi_ref, new_lo_ref, new_hi_ref, nv_ref, x_ref,
                wg_lo, wg_hi, wu_lo, wu_hi, wd_lo, wd_hi, g_ref, b_ref, o_ref, wg_scr, wu_scr, wd_scr):
    t = pl.program_id(0)
    nv = nv_ref[t]

    for j, (flag_ref, wg, wu, wd) in enumerate(((new_lo_ref, wg_lo, wu_lo, wd_lo), (new_hi_ref, wg_hi, wu_hi, wd_hi))):
        @pl.when(flag_ref[t] > 0)
        def _():
            wg_scr[j] = wg[...].astype(BF16)
            wu_scr[j] = wu[...].astype(BF16)
            wd_scr[j] = wd[...].astype(BF16)

    @pl.when(nv > 0)
    def _():
        tm, d = o_ref.shape
        live = lax.broadcasted_iota(jnp.int32, (tm, 1), 0) < nv
        x = jnp.where(live, x_ref[:, 0:d], 0.0)
        cw = jnp.where(live, x_ref[:, d:], 0.0)
        xb = x.astype(BF16)
        acc = ALPHA * x
        for j in range(TOP_K):
            gate = jnp.dot(xb, wg_scr[j], preferred_element_type=F32)
            up = jnp.dot(xb, wu_scr[j], preferred_element_type=F32)
            act = gate * jax.nn.sigmoid(gate) * up * cw[:, j:j + 1]
            acc = acc + jnp.dot(act.astype(BF16), wd_scr[j], preferred_element_type=F32)
        o_ref[...] = _layer_norm(acc, g_ref[...], b_ref[...])


def _moe(x_sorted, tables, wg, wu, wd, layer, g, b, tm, name):
    rows, wa = x_sorted.shape
    d = wa - LANES
    n_tiles = tables[0].shape[0]
    rows_of = lambda t, blk, e_lo, e_hi, new_lo, new_hi, nv: (blk[t], 0)
    lo = lambda t, blk, e_lo, e_hi, new_lo, new_hi, nv: (layer, e_lo[t], 0, 0)
    hi = lambda t, blk, e_lo, e_hi, new_lo, new_hi, nv: (layer, e_hi[t], 0, 0)
    const = lambda t, blk, e_lo, e_hi, new_lo, new_hi, nv: (0, 0)
    w_in = lambda f: pl.BlockSpec((None, None, d, D_EXPERT), f)
    w_out = lambda f: pl.BlockSpec((None, None, D_EXPERT, d), f)
    return pl.pallas_call(
        _moe_kernel,
        grid_spec=pltpu.PrefetchScalarGridSpec(
            num_scalar_prefetch=6,
            grid=(n_tiles,),
            in_specs=[pl.BlockSpec((tm, wa), rows_of),
                      w_in(lo), w_in(hi), w_in(lo), w_in(hi), w_out(lo), w_out(hi),
                      pl.BlockSpec((1, d), const), pl.BlockSpec((1, d), const)],
            out_specs=pl.BlockSpec((tm, d), rows_of),
            scratch_shapes=[pltpu.VMEM((TOP_K, d, D_EXPERT), BF16), pltpu.VMEM((TOP_K, d, D_EXPERT), BF16),
                            pltpu.VMEM((TOP_K, D_EXPERT, d), BF16)]),
        out_shape=jax.ShapeDtypeStruct((rows, d), F32),
        compiler_params=_cparams(("arbitrary",)),
        name=name,
    )(*tables, x_sorted, wg, wg, wu, wu, wd, wd, g, b)


def _bucket_layout(counts, tm, n_tiles):
    tiles_b = (counts + tm - 1) // tm
    ends = jnp.cumsum(tiles_b)
    starts = ends - tiles_b
    t = jnp.arange(n_tiles, dtype=jnp.int32)
    tc = jnp.minimum(t, ends[-1] - 1)
    bkt = jnp.sum((tc[:, None] >= ends[None, :]).astype(jnp.int32), axis=1)
    nv = jnp.where(t < ends[-1], jnp.clip(counts[bkt] - (tc - starts[bkt]) * tm, 0, tm), 0)
    pair_lo = jnp.array([p[0] for p in PAIRS], jnp.int32)
    pair_hi = jnp.array([p[1] for p in PAIRS], jnp.int32)
    grp, pair = bkt // len(PAIRS), bkt % len(PAIRS)
    e_lo = grp * EXPERTS_PER_GROUP + pair_lo[pair]
    e_hi = grp * EXPERTS_PER_GROUP + pair_hi[pair]
    changed = lambda e: jnp.concatenate([jnp.ones((1,), jnp.int32), (e[1:] != e[:-1]).astype(jnp.int32)])
    i32 = lambda a: a.astype(jnp.int32)
    return i32(starts * tm), (i32(tc), i32(e_lo), i32(e_hi), changed(e_lo), changed(e_hi), i32(nv))


def _routed_moe(routed, wg, wu, wd, layer, g, b, out_idx_fn, n_out, name):
    haug, bucket, rank, cnt = routed
    n = haug.shape[0]
    tm = _pick(n, (256, 128))
    n_tiles = n // tm + N_BUCKETS
    counts = cnt[:N_BUCKETS, 0].astype(jnp.int32)
    row0, tables = _bucket_layout(counts, tm, n_tiles)
    pos = row0[bucket.reshape(n)] + rank.reshape(n)
    x_sorted = _permute_rows(_scatter_rows_kernel, pos, haug, n_tiles * tm, name + "_scatter")
    y_sorted = _moe(x_sorted, tables, wg, wu, wd, layer, g, b, tm, name)
    return _permute_rows(_gather_rows_kernel, out_idx_fn(pos), y_sorted, n_out, name + "_gather")


def kernel(x, meta, even_w_in, pool_w, pool_scale, diff_lq1, diff_lk1, diff_lq2, diff_lk2, diff_subln,
           even_w_out, odd_w_in, gla_gate_w2, gla_gate_b, gla_head_norm, odd_w_out, ln_mix_g, ln_mix_b,
           ln_ffn_g, ln_ffn_b, router_w, router_bias, moe_w_gate, moe_w_up, moe_w_down):
    bsz, seq, d = x.shape
    l = PREFIX + seq
    n = bsz * l
    tm = 5 * PREFIX
    assert l % tm == 0
    prefix = jnp.concatenate([jnp.zeros((FIRST_VALID, d), x.dtype), meta.astype(x.dtype)], axis=0)
    h = (x, prefix)
    router_w_t = router_w.T
    row = lambda a: a.reshape(1, -1)

    for i in range(DEPTH):
        j = i // 2
        if i % 2 == 0:
            lambda_init = 0.8 - 0.6 * math.exp(-0.3 * i)
            w_in = even_w_in[j].astype(BF16)
            z = _proj_stream(*h, w_in, tm, "even_in_proj") if isinstance(h, tuple) else _proj(h, w_in, "even_in_proj")
            z = z.reshape(bsz, l, EVEN_IN)
            y_pool = _pool(z, pool_w[j].astype(BF16), row(pool_scale[j]))
            lam_params = jnp.stack([diff_lq1[j], diff_lk1[j], diff_lq2[j], diff_lk2[j]])
            y_attn = _attn(z, lam_params, diff_subln[j].reshape(-1, 1), lambda_init)
            ys = [y_pool.reshape(n, POOL_WIDTH), y_attn.reshape(n, DIFF_WIDTH)]
            routed = _mixout_route(ys, even_w_out[j].astype(BF16), h, row(ln_mix_g[i]), row(ln_mix_b[i]),
                                   router_w_t, router_bias, tm, "even_out_proj")
        else:
            w_in = jnp.pad(odd_w_in[j], ((0, 0), (0, ODD_IN_PAD - ODD_IN))).astype(BF16)
            z = _proj(h, w_in, "odd_in_proj").reshape(bsz, l, ODD_IN_PAD)
            w2 = jnp.pad(gla_gate_w2[j], ((0, LANES - GLA_RANK), (0, 0))).astype(BF16)
            y = _gla(z, w2, row(gla_gate_b[j]), row(gla_head_norm[j]))
            routed = _mixout_route([y.reshape(n, GLA_V_WIDTH)], odd_w_out[j].astype(BF16), h,
                                   row(ln_mix_g[i]), row(ln_mix_b[i]), router_w_t, router_bias, tm, "odd_out_proj")
        last = i == DEPTH - 1
        if last:
            out_idx_fn = lambda pos: pos.reshape(bsz, l)[:, PREFIX:].reshape(-1)
        else:
            out_idx_fn = lambda pos: pos
        h = _routed_moe(routed, moe_w_gate, moe_w_up, moe_w_down, i, row(ln_ffn_g[i]), row(ln_ffn_b[i]),
                        out_idx_fn, bsz * seq if last else n, f"moe_{i}")
    return h.reshape(bsz, seq, d)
```

```python
import functools
import math

import jax
import jax.numpy as jnp
from jax import lax
from jax.experimental import pallas as pl
from jax.experimental.pallas import tpu as pltpu

F32 = jnp.float32
BF16 = jnp.bfloat16

D_MODEL = 1024
DEPTH = 2
N_META = 16
PREFIX = 128
FIRST_VALID = PREFIX - N_META
POOL_WINDOWS = (2, 4, 8, 16)
POOL_GROUP = 128
POOL_WIDTH = 512
MAX_WINDOW = 16
DIFF_HEADS = 4
DIFF_QK_DIM = 64
DIFF_V_DIM = 128
DIFF_WIDTH = 512
EVEN_IN = 2048
GLA_HEADS = 4
GLA_DK = 128
GLA_DV = 256
GLA_RANK = 16
GLA_TAU = 16.0
GLA_QK_WIDTH = GLA_HEADS * GLA_DK
GLA_V_WIDTH = GLA_HEADS * GLA_DV
ODD_IN = 2 * GLA_QK_WIDTH + 2 * GLA_V_WIDTH + GLA_RANK
ODD_IN_PAD = 3200
N_EXPERTS = 16
N_GROUPS = 4
EXPERTS_PER_GROUP = 4
D_EXPERT = 512
TOP_K = 2
PAIRS = ((0, 1), (0, 2), (0, 3), (1, 2), (1, 3), (2, 3))
N_BUCKETS = N_GROUPS * len(PAIRS)
BUCKET_ROWS = 32
ALPHA = (2.0 * DEPTH) ** 0.25
NEG_BIG = -1e30
LOG2E = math.log2(math.e)
KT = 640
ONES_ROWS = 16

LANES = 128
VMEM_LIMIT = 56 * 1024 * 1024


def _pick(n, candidates):
    for c in candidates:
        if n % c == 0:
            return c
    raise ValueError(f"no tile for {n} in {candidates}")


def _cparams(sem):
    return pltpu.CompilerParams(dimension_semantics=sem, vmem_limit_bytes=VMEM_LIMIT)


def _layer_norm(a, g, b):
    mu = jnp.mean(a, axis=-1, keepdims=True)
    d = a - mu
    var = jnp.mean(d * d, axis=-1, keepdims=True)
    return d * lax.rsqrt(var + 1e-5) * g + b


def _proj_kernel(x_ref, w_ref, o_ref, *, tn):
    xb = x_ref[...].astype(BF16)
    for j in range(o_ref.shape[1] // tn):
        cols = slice(j * tn, (j + 1) * tn)
        o_ref[:, cols] = jnp.dot(xb, w_ref[:, cols], preferred_element_type=F32).astype(o_ref.dtype)


def _stream_tile(main_ref, prev_ref, prefix_ref, t):
    head = jnp.where(t == 0, prefix_ref[...], prev_ref[...])
    return jnp.concatenate([head, main_ref[0:main_ref.shape[0] - PREFIX, :]], axis=0)


def _stream_specs(seq, d, tm):
    tiles = (PREFIX + seq) // tm
    per = tm // PREFIX
    main = pl.BlockSpec((None, tm, d), lambda i: (i // tiles, i % tiles, 0))
    prev = pl.BlockSpec((None, PREFIX, d), lambda i: (i // tiles, jnp.maximum((i % tiles) * per - 1, 0), 0))
    prefix = pl.BlockSpec((PREFIX, d), lambda i: (0, 0))
    return tiles, [main, prev, prefix]


def _proj_stream_kernel(main_ref, prev_ref, prefix_ref, w_ref, o_ref, *, tn, tiles):
    xb = _stream_tile(main_ref, prev_ref, prefix_ref, pl.program_id(0) % tiles).astype(BF16)
    for j in range(o_ref.shape[1] // tn):
        cols = slice(j * tn, (j + 1) * tn)
        o_ref[:, cols] = jnp.dot(xb, w_ref[:, cols], preferred_element_type=F32).astype(o_ref.dtype)


def _proj_stream(x, prefix, w, tm, name):
    bsz, seq, k = x.shape
    wout = w.shape[1]
    tn = _pick(wout, (640, 512, 128))
    tiles, stream = _stream_specs(seq, k, tm)
    return pl.pallas_call(
        functools.partial(_proj_stream_kernel, tn=tn, tiles=tiles),
        grid=(bsz * tiles,),
        in_specs=stream + [pl.BlockSpec((k, wout), lambda i: (0, 0))],
        out_specs=pl.BlockSpec((tm, wout), lambda i: (i, 0)),
        out_shape=jax.ShapeDtypeStruct((bsz * tiles * tm, wout), BF16),
        compiler_params=_cparams(("parallel",)),
        name=name,
    )(x, x, prefix, w)


def _proj(x2d, w, name):
    n, k = x2d.shape
    wout = w.shape[1]
    tm = _pick(n, (640, 512, 256, 128))
    tn = _pick(wout, (640, 512, 128))
    return pl.pallas_call(
        functools.partial(_proj_kernel, tn=tn),
        grid=(n // tm,),
        in_specs=[pl.BlockSpec((tm, k), lambda i: (i, 0)),
                  pl.BlockSpec((k, wout), lambda i: (0, 0))],
        out_specs=pl.BlockSpec((tm, wout), lambda i: (i, 0)),
        out_shape=jax.ShapeDtypeStruct((n, wout), BF16),
        compiler_params=_cparams(("parallel",)),
        name=name,
    )(x2d, w)


def _pool_kernel(cur_ref, halo_ref, w_ref, sc_ref, o_ref, u_scr):
    t = pl.program_id(1)
    tl = cur_ref.shape[0]
    pos = t * tl + lax.broadcasted_iota(jnp.int32, (tl, 1), 0)
    hpos = t * tl - MAX_WINDOW + lax.broadcasted_iota(jnp.int32, (MAX_WINDOW, 1), 0)
    u_scr[0:MAX_WINDOW, :] = jnp.where(hpos >= FIRST_VALID, halo_ref[...].astype(F32), 0.0)
    u_scr[MAX_WINDOW:, :] = jnp.where(pos >= FIRST_VALID, cur_ref[...].astype(F32), 0.0)
    n_valid = pos - (FIRST_VALID - 1)
    for gi, w in enumerate(POOL_WINDOWS):
        cols = slice(gi * POOL_GROUP, (gi + 1) * POOL_GROUP)
        u = u_scr[MAX_WINDOW:MAX_WINDOW + tl, cols]
        win = u
        for j in range(1, w):
            win = win + u_scr[MAX_WINDOW - j:MAX_WINDOW - j + tl, cols]
        cnt = jnp.clip(n_valid, 1, w).astype(F32)
        d = win / cnt - u
        y = jnp.dot(d.astype(BF16), w_ref[gi], preferred_element_type=F32) * sc_ref[:, cols]
        o_ref[:, cols] = y.astype(o_ref.dtype)


def _pool(z, pool_w, pool_scale):
    b, l, _ = z.shape
    tl = _pick(l, (640, 128))
    hb = tl // MAX_WINDOW
    return pl.pallas_call(
        _pool_kernel,
        grid=(b, l // tl),
        in_specs=[pl.BlockSpec((None, tl, POOL_WIDTH), lambda bi, t: (bi, t, 0)),
                  pl.BlockSpec((None, MAX_WINDOW, POOL_WIDTH),
                               lambda bi, t: (bi, jnp.maximum(t * hb - 1, 0), 0)),
                  pl.BlockSpec((len(POOL_WINDOWS), POOL_GROUP, POOL_GROUP), lambda bi, t: (0, 0, 0)),
                  pl.BlockSpec((1, POOL_WIDTH), lambda bi, t: (0, 0))],
        out_specs=pl.BlockSpec((None, tl, POOL_WIDTH), lambda bi, t: (bi, t, 0)),
        out_shape=jax.ShapeDtypeStruct((b, l, POOL_WIDTH), BF16),
        scratch_shapes=[pltpu.VMEM((tl + MAX_WINDOW, POOL_WIDTH), F32)],
        compiler_params=_cparams(("parallel", "parallel")),
        name="pool_mixer",
    )(z, z, pool_w, pool_scale)


def _attn_kernel(lam_ref, q_ref, k_ref, v_ref, sub_ref, o_ref, ka_scr, kb_scr, vt_scr,
                 s00_scr, s01_scr, s10_scr, s11_scr, t00_scr, t01_scr, t10_scr, t11_scr,
                 m0_scr, m1_scr, l0_scr, l1_scr, acc0_scr, acc1_scr, *, tq, lambda_init):
    h = pl.program_id(1)
    i = pl.program_id(2)
    slope = jnp.exp2(-2.0 * (h + 1).astype(F32))
    half = DIFF_QK_DIM
    n_chunks = k_ref.shape[0] // LANES
    n_tail, n_pad = _attn_tail(tq)

    @pl.when(i == 0)
    def _():
        lane = lax.broadcasted_iota(jnp.int32, (LANES, 2 * half), 1)
        rowf = lax.broadcasted_iota(jnp.int32, (LANES, 2 * half), 0).astype(F32)

        def build(c, carry):
            rows = pl.ds(pl.multiple_of(c * LANES, LANES), LANES)
            kt = k_ref[rows, :].astype(F32)
            hi = jnp.full((LANES, 2 * half), c, jnp.int32).astype(F32)
            ka = jnp.where(lane < half, kt, jnp.where(lane == half, hi, jnp.where(lane == half + 1, rowf, 0.0)))
            kb = jnp.where(lane >= half, kt, jnp.where(lane == 0, hi, jnp.where(lane == 1, rowf, 0.0)))
            ka_scr[rows, :] = ka.astype(BF16)
            kb_scr[rows, :] = kb.astype(BF16)
            vt_scr[c, 0:DIFF_V_DIM, :] = v_ref[rows, :].astype(F32).T.astype(BF16)
            vt_scr[c, DIFF_V_DIM:, :] = jnp.ones((ONES_ROWS, LANES), BF16)
            return carry

        lax.fori_loop(0, n_chunks, build, 0)
        for extra in range(n_pad):
            pad_rows = pl.ds((n_chunks + extra) * LANES, LANES)
            ka_scr[pad_rows, :] = jnp.zeros((LANES, 2 * half), BF16)
            kb_scr[pad_rows, :] = jnp.zeros((LANES, 2 * half), BF16)
            vt_scr[n_chunks + extra] = jnp.zeros((DIFF_V_DIM + ONES_ROWS, LANES), BF16)

    lane = lax.broadcasted_iota(jnp.int32, (tq, 2 * half), 1)
    q = (q_ref[...] * jnp.asarray(half ** -0.5, BF16)).astype(F32)
    f_hi = LANES * slope
    qa = jnp.where(lane < half, q, jnp.where(lane == half, f_hi, jnp.where(lane == half + 1, slope, 0.0)))
    qb = jnp.where(lane >= half, q, jnp.where(lane == 0, f_hi, jnp.where(lane == 1, slope, 0.0)))
    q_maps = (qa.T.astype(BF16), qb.T.astype(BF16))
    k_maps = (ka_scr, kb_scr)
    s_scrs = ((s00_scr, s01_scr), (s10_scr, s11_scr))
    t_scrs = ((t00_scr, t01_scr), (t10_scr, t11_scr))
    m_scrs, l_scrs, acc_scrs = (m0_scr, m1_scr), (l0_scr, l1_scr), (acc0_scr, acc1_scr)
    for mp in range(2):
        m_scrs[mp][...] = jnp.full(m0_scr.shape, NEG_BIG, F32)
        l_scrs[mp][...] = jnp.zeros(l0_scr.shape, F32)
        acc_scrs[mp][...] = jnp.zeros(acc0_scr.shape, F32)
    q0 = i * tq
    n_full = q0 // KT

    def scores(t, mode, buf):
        k0 = pl.multiple_of(t * KT, KT)
        allowed = None
        if mode is not None:
            kpos = k0 + lax.broadcasted_iota(jnp.int32, (KT, tq), 0)
            allowed = kpos >= FIRST_VALID
            if mode == "diag":
                qpos = q0 + lax.broadcasted_iota(jnp.int32, (KT, tq), 1)
                allowed = jnp.logical_and(kpos <= qpos, jnp.logical_or(allowed, kpos == qpos))
        for mp in range(2):
            s = jnp.dot(k_maps[mp][pl.ds(k0, KT), :], q_maps[mp], preferred_element_type=F32)
            if allowed is not None:
                s = jnp.where(allowed, s, NEG_BIG)
            s_scrs[mp][buf][...] = s
            t_scrs[mp][buf][...] = jnp.max(s, axis=0, keepdims=True)

    def accumulate(t, buf):
        c0 = t * (KT // LANES)
        vt = jnp.concatenate([vt_scr[c0 + u] for u in range(KT // LANES)], axis=1)
        for mp in range(2):
            m_prev = m_scrs[mp][...]
            m_new = jnp.maximum(m_prev, t_scrs[mp][buf][...])
            alpha = jnp.exp(m_prev - m_new)
            p = jnp.exp(s_scrs[mp][buf][...] - m_new)
            pv = jnp.dot(vt, p.astype(BF16), preferred_element_type=F32)
            l_scrs[mp][...] = alpha * l_scrs[mp][...] + pv[DIFF_V_DIM:DIFF_V_DIM + 1, :]
            acc_scrs[mp][...] = alpha * acc_scrs[mp][...] + pv[0:DIFF_V_DIM, :]
            m_scrs[mp][...] = m_new

    assert n_tail == 1
    pairs = jnp.maximum(n_full - 1, 0) // 2
    rest = n_full - 1 - 2 * pairs

    @pl.when(n_full >= 1)
    def _():
        scores(0, "valid", 0)

    def steady(u, c):
        accumulate(2 * u, 0)
        scores(2 * u + 1, None, 1)
        accumulate(2 * u + 1, 1)
        scores(2 * u + 2, None, 0)
        return c

    lax.fori_loop(0, pairs, steady, 0)

    @pl.when(n_full == 0)
    def _():
        scores(0, "diag", 0)
        accumulate(0, 0)

    @pl.when(jnp.logical_and(n_full >= 1, rest == 0))
    def _():
        accumulate(n_full - 1, 0)
        scores(n_full, "diag", 1)
        accumulate(n_full, 1)

    @pl.when(jnp.logical_and(n_full >= 1, rest == 1))
    def _():
        accumulate(n_full - 2, 0)
        scores(n_full - 1, None, 1)
        accumulate(n_full - 1, 1)
        scores(n_full, "diag", 0)
        accumulate(n_full, 0)

    lam = (jnp.exp(jnp.sum(lam_ref[0:1, :] * lam_ref[1:2, :], axis=-1, keepdims=True))
           - jnp.exp(jnp.sum(lam_ref[2:3, :] * lam_ref[3:4, :], axis=-1, keepdims=True)) + lambda_init)
    o = acc0_scr[...] / l0_scr[...] - lam * (acc1_scr[...] / l1_scr[...])
    o = o * lax.rsqrt(jnp.mean(o * o, axis=0, keepdims=True) + 1e-6) * sub_ref[...] * (1.0 - lambda_init)
    o_ref[...] = o.T.astype(o_ref.dtype)


def _attn_tail(tq):
    n_tail = -(-(tq + KT - math.gcd(tq, KT)) // KT)
    return n_tail, (n_tail * KT - tq) // LANES


def _attn(z, lam_params, subln_col, lambda_init):
    b, l, _ = z.shape
    tq = 5 * LANES
    assert l % tq == 0 and tq % LANES == 0 and KT % LANES == 0
    n_pad = _attn_tail(tq)[1]
    qb, kb, vb = POOL_WIDTH // LANES, (POOL_WIDTH + DIFF_WIDTH) // LANES, (POOL_WIDTH + 2 * DIFF_WIDTH) // LANES
    row = pltpu.VMEM((1, tq), F32)
    acc = pltpu.VMEM((DIFF_V_DIM, tq), F32)
    sbuf = pltpu.VMEM((KT, tq), F32)
    return pl.pallas_call(
        functools.partial(_attn_kernel, tq=tq, lambda_init=lambda_init),
        grid=(b, DIFF_HEADS, l // tq),
        in_specs=[pl.BlockSpec((4, DIFF_QK_DIM), lambda bi, h, i: (0, 0)),
                  pl.BlockSpec((None, tq, LANES), lambda bi, h, i: (bi, i, qb + h)),
                  pl.BlockSpec((None, l, LANES), lambda bi, h, i: (bi, 0, kb + h)),
                  pl.BlockSpec((None, l, LANES), lambda bi, h, i: (bi, 0, vb + h)),
                  pl.BlockSpec((DIFF_V_DIM, 1), lambda bi, h, i: (0, 0))],
        out_specs=pl.BlockSpec((None, tq, DIFF_V_DIM), lambda bi, h, i: (bi, i, h)),
        out_shape=jax.ShapeDtypeStruct((b, l, DIFF_WIDTH), BF16),
        scratch_shapes=[pltpu.VMEM((l + n_pad * LANES, LANES), BF16), pltpu.VMEM((l + n_pad * LANES, LANES), BF16),
                        pltpu.VMEM((l // LANES + n_pad, DIFF_V_DIM + ONES_ROWS, LANES), BF16),
                        sbuf, sbuf, sbuf, sbuf, row, row, row, row, row, row, row, row, acc, acc],
        compiler_params=_cparams(("parallel", "parallel", "arbitrary")),
        name="diff_attn",
    )(lam_params, z, z, z, subln_col)


def _gla_kernel(q_ref, k_ref, v_ref, r_ref, glr_ref, w2_ref, gb_ref, hn_ref, o_ref, st_ref, *, c_len, sb):
    c = pl.program_id(0)
    n_batch = q_ref.shape[0]

    @pl.when(c == 0)
    def _():
        st_ref[...] = jnp.zeros(st_ref.shape, F32)

    pos = c * c_len + lax.broadcasted_iota(jnp.int32, (c_len, 1), 0)
    validf = (pos >= FIRST_VALID).astype(F32)
    ri = lax.broadcasted_iota(jnp.int32, (c_len, c_len), 0)
    ci = lax.broadcasted_iota(jnp.int32, (c_len, c_len), 1)
    tri = jnp.where(ri >= ci, 1.0, 0.0).astype(BF16)
    row = lax.broadcasted_iota(jnp.int32, (c_len, 1), 0)
    t_loc = lax.broadcasted_iota(jnp.int32, (sb, 1), 0)
    lane_c = lax.broadcasted_iota(jnp.int32, (sb, c_len), 1)
    nt = (((1,), (1,)), ((), ()))
    tn = (((0,), (0,)), ((), ()))

    for bi, h in [(bi, h) for bi in range(n_batch) for h in range(GLA_HEADS)]:
        glr = glr_ref[bi]
        ks = slice(h * GLA_DK, (h + 1) * GLA_DK)
        vs = slice(h * GLA_DV, (h + 1) * GLA_DV)
        g = jnp.dot(glr, w2_ref[:, ks], preferred_element_type=F32) + gb_ref[:, ks]
        log_a = -(jnp.maximum(-g, 0.0) + jnp.log(1.0 + jnp.exp(-jnp.abs(g)))) * (1.0 / GLA_TAU)
        la_hi = log_a.astype(BF16)
        la_lo = (log_a - la_hi.astype(F32)).astype(BF16)
        b = (jnp.dot(tri, la_hi, preferred_element_type=F32)
             + jnp.dot(tri, la_lo, preferred_element_type=F32))
        q = q_ref[bi, :, ks].astype(F32) * (GLA_DK ** -0.5)
        k = k_ref[bi, :, ks].astype(F32) * validf
        v = v_ref[bi, :, vs]
        st = st_ref[bi, h]

        b2 = b * LOG2E
        blocks = []
        for i in range(c_len // sb):
            lo = i * sb
            q_i = q[lo:lo + sb]
            b_i = b[lo:lo + sb]
            b2_i = b2[lo:lo + sb]
            a_i = jnp.zeros((sb, c_len), F32)
            if i > 0:
                b_ref_row = b[lo - 1:lo]
                q_t = q_i * jnp.exp(b_i - b_ref_row)
                k_t = jnp.where(row < lo, k * jnp.exp(jnp.minimum(b_ref_row - b, 0.0)), 0.0)
                a_i = lax.dot_general(q_t.astype(BF16), k_t.astype(BF16), nt, preferred_element_type=F32)
            for s in range(sb):
                r = lo + s
                e = jnp.exp2(b2_i - b2[r:r + 1])
                col = jnp.sum(q_i * k[r:r + 1] * e, axis=-1, keepdims=True)
                a_i = jnp.where(lane_c == r, col, a_i)
            blocks.append(jnp.where(lane_c <= lo + t_loc, a_i, 0.0))
        att = jnp.concatenate(blocks, axis=0)

        o = jnp.dot(att.astype(BF16), v, preferred_element_type=F32)
        o = o + lax.dot_general((q * jnp.exp(b)).astype(BF16), st.astype(BF16), nt,
                                preferred_element_type=F32)
        b_last = b[c_len - 1:c_len]
        k_hat = (k * jnp.exp(b_last - b)).astype(BF16)
        st_ref[bi, h] = st * jnp.exp(b_last) + lax.dot_general(v, k_hat, tn, preferred_element_type=F32)

        o = o * lax.rsqrt(jnp.mean(o * o, axis=-1, keepdims=True) + 1e-6) * hn_ref[...]
        rg = r_ref[bi, :, vs].astype(F32)
        o_ref[bi, :, vs] = (o * (rg * jax.nn.sigmoid(rg))).astype(o_ref.dtype)


def _gla(z, w2, gate_b, head_norm):
    b, l, _ = z.shape
    c_len = 128
    qw, vw = GLA_QK_WIDTH, GLA_V_WIDTH
    return pl.pallas_call(
        functools.partial(_gla_kernel, c_len=c_len, sb=16),
        grid=(l // c_len,),
        in_specs=[pl.BlockSpec((b, c_len, qw), lambda c: (0, c, 0)),
                  pl.BlockSpec((b, c_len, qw), lambda c: (0, c, 1)),
                  pl.BlockSpec((b, c_len, vw), lambda c: (0, c, 1)),
                  pl.BlockSpec((b, c_len, vw), lambda c: (0, c, 2)),
                  pl.BlockSpec((b, c_len, LANES), lambda c: (0, c, (2 * qw + 2 * vw) // LANES)),
                  pl.BlockSpec((LANES, qw), lambda c: (0, 0)),
                  pl.BlockSpec((1, qw), lambda c: (0, 0)),
                  pl.BlockSpec((1, GLA_DV), lambda c: (0, 0))],
        out_specs=pl.BlockSpec((b, c_len, vw), lambda c: (0, c, 0)),
        out_shape=jax.ShapeDtypeStruct((b, l, vw), BF16),
        scratch_shapes=[pltpu.VMEM((b, GLA_HEADS, GLA_DV, GLA_DK), F32)],
        compiler_params=_cparams(("arbitrary",)),
        name="gla_mixer",
    )(z, z, z, z, z, w2, gate_b, head_norm)


def _route_tokens(x, bias_ref, rw_ref, haug_ref, bucket_ref, rank_ref, cnt_ref, upper_scr, cnt_scr):
    tm, d = x.shape

    @pl.when(pl.program_id(0) == 0)
    def _():
        r = lax.broadcasted_iota(jnp.int32, (tm, tm), 0)
        c = lax.broadcasted_iota(jnp.int32, (tm, tm), 1)
        upper_scr[...] = jnp.where(r < c, 1.0, 0.0).astype(BF16)
        cnt_scr[...] = jnp.zeros(cnt_scr.shape, F32)

    w = rw_ref[...]
    xh = x.astype(BF16)
    xl = (x - xh.astype(F32)).astype(BF16)
    wh = w.astype(BF16)
    wl = (w - wh.astype(F32)).astype(BF16)
    nt = (((1,), (1,)), ((), ()))
    logits = (lax.dot_general(wh, xh, nt, preferred_element_type=F32)
              + lax.dot_general(wh, xl, nt, preferred_element_type=F32)
              + lax.dot_general(wl, xh, nt, preferred_element_type=F32))
    lg = [logits[e:e + 1, :] for e in range(N_EXPERTS)]
    mx = functools.reduce(jnp.maximum, lg)
    ex = [jnp.exp(v - mx) for v in lg]
    den = functools.reduce(jnp.add, ex)
    probs = [v / den for v in ex]
    sel = [probs[e] + bias_ref[e] for e in range(N_EXPERTS)]

    def top2_sum(a, b, c, d_):
        hi1, lo1 = jnp.maximum(a, b), jnp.minimum(a, b)
        hi2, lo2 = jnp.maximum(c, d_), jnp.minimum(c, d_)
        return jnp.maximum(hi1, hi2) + jnp.maximum(jnp.minimum(hi1, hi2), jnp.maximum(lo1, lo2))

    best = top2_sum(*sel[0:4])
    gidx = jnp.zeros((1, tm), jnp.int32)
    for g in range(1, N_GROUPS):
        sc = top2_sum(*sel[4 * g:4 * g + 4])
        better = sc > best
        gidx = jnp.where(better, g, gidx)
        best = jnp.maximum(best, sc)

    def pick(vals, j):
        out = vals[j]
        for g in range(1, N_GROUPS):
            out = jnp.where(gidx == g, vals[4 * g + j], out)
        return out

    sg = [pick(sel, j) for j in range(EXPERTS_PER_GROUP)]
    pg = [pick(probs, j) for j in range(EXPERTS_PER_GROUP)]
    picked = []
    for j in range(EXPERTS_PER_GROUP):
        rank = jnp.zeros((1, tm), jnp.int32)
        for o in range(EXPERTS_PER_GROUP):
            if o == j:
                continue
            ahead = (sg[o] > sg[j]) if o > j else (sg[o] >= sg[j])
            rank = rank + ahead.astype(jnp.int32)
        picked.append(rank < TOP_K)
    chosen = [jnp.where(picked[j], pg[j], 0.0) for j in range(EXPERTS_PER_GROUP)]
    wsum = functools.reduce(jnp.add, chosen)
    wn = [cj / wsum for cj in chosen]

    pair = jnp.where(picked[0],
                     jnp.where(picked[1], 0, jnp.where(picked[2], 1, 2)),
                     jnp.where(picked[1], jnp.where(picked[2], 3, 4), 5))
    w_lo = jnp.where(picked[0], wn[0], jnp.where(picked[1], wn[1], wn[2]))
    w_hi = jnp.where(picked[3], wn[3], jnp.where(picked[2], wn[2], wn[1]))
    bucket = gidx * len(PAIRS) + pair

    onehot = jnp.where(lax.broadcasted_iota(jnp.int32, (BUCKET_ROWS, tm), 0) == bucket, 1.0, 0.0)
    before = jnp.dot(onehot.astype(BF16), upper_scr[...], preferred_element_type=F32)
    seen = before + cnt_scr[:, 0:1]
    bucket_ref[...] = bucket
    rank_ref[...] = jnp.sum(onehot * seen, axis=0, keepdims=True).astype(jnp.int32)
    cnt_scr[...] = cnt_scr[...] + jnp.sum(onehot, axis=1, keepdims=True)
    cnt_ref[...] = cnt_scr[...]

    pay_t = jnp.concatenate([w_lo, w_hi, jnp.zeros((LANES - TOP_K, tm), F32)], axis=0)
    haug_ref[:, 0:d] = x
    haug_ref[:, d:] = pay_t.T


def _mixout_route_kernel(*refs, stream_tiles):
    if stream_tiles:
        bias_ref, *y_refs, w_ref, main_ref, prev_ref, prefix_ref, g_ref, b_ref, rw_ref = refs[:-6]
        resid = _stream_tile(main_ref, prev_ref, prefix_ref, pl.program_id(0) % stream_tiles)
    else:
        bias_ref, *y_refs, w_ref, h_ref, g_ref, b_ref, rw_ref = refs[:-6]
        resid = h_ref[...]
    acc = ALPHA * resid
    off = 0
    for y_ref in y_refs:
        kd = y_ref.shape[1]
        acc = acc + jnp.dot(y_ref[...], w_ref[off:off + kd, :], preferred_element_type=F32)
        off += kd
    _route_tokens(_layer_norm(acc, g_ref[...], b_ref[...]), bias_ref, rw_ref, *refs[-6:])


def _mixout_route(ys, w, resid, g, b, router_w_t, router_bias, tm, name):
    n = ys[0].shape[0]
    d = w.shape[1]
    rows = lambda width: pl.BlockSpec((tm, width), lambda i: (i, 0))
    whole = lambda a: pl.BlockSpec(a.shape, lambda i: (0, 0))
    per_tile = pl.BlockSpec((None, 1, tm), lambda i: (i, 0, 0))
    if isinstance(resid, tuple):
        x, prefix = resid
        stream_tiles, resid_specs = _stream_specs(x.shape[1], d, tm)
        resid_args = (x, x, prefix)
    else:
        stream_tiles, resid_specs, resid_args = 0, [rows(d)], (resid,)
    return pl.pallas_call(
        functools.partial(_mixout_route_kernel, stream_tiles=stream_tiles),
        grid=(n // tm,),
        in_specs=[pl.BlockSpec(memory_space=pltpu.SMEM)] + [rows(y.shape[1]) for y in ys]
        + [whole(w)] + resid_specs + [whole(g), whole(b), whole(router_w_t)],
        out_specs=[rows(d + LANES), per_tile, per_tile, pl.BlockSpec((BUCKET_ROWS, LANES), lambda i: (0, 0))],
        out_shape=[jax.ShapeDtypeStruct((n, d + LANES), F32),
                   jax.ShapeDtypeStruct((n // tm, 1, tm), jnp.int32),
                   jax.ShapeDtypeStruct((n // tm, 1, tm), jnp.int32),
                   jax.ShapeDtypeStruct((BUCKET_ROWS, LANES), F32)],
        scratch_shapes=[pltpu.VMEM((tm, tm), BF16), pltpu.VMEM((BUCKET_ROWS, LANES), F32)],
        compiler_params=_cparams(("arbitrary",)),
        name=name,
    )(router_bias, *ys, w, *resid_args, g, b, router_w_t)


def _row_dma(src_ref, dst_ref, sem, src_row, dst_row):
    return pltpu.make_async_copy(src_ref.at[pl.ds(src_row, 1), :], dst_ref.at[pl.ds(dst_row, 1), :], sem)


def _scatter_rows_kernel(idx_ref, src_ref, dst_ref, sem, *, rows):
    base = pl.program_id(0) * rows
    for r in range(rows):
        _row_dma(src_ref, dst_ref, sem, r, idx_ref[base + r]).start(priority=r % 2)

    def drain(r, c):
        _row_dma(src_ref, dst_ref, sem, 0, 0).wait()
        return c

    lax.fori_loop(0, rows, drain, 0, unroll=8)


def _gather_rows_kernel(idx_ref, src_ref, dst_ref, sem, *, rows):
    base = pl.program_id(0) * rows
    for r in range(rows):
        _row_dma(src_ref, dst_ref, sem, idx_ref[base + r], r).start(priority=r % 2)

    def drain(r, c):
        _row_dma(src_ref, dst_ref, sem, 0, 0).wait()
        return c

    lax.fori_loop(0, rows, drain, 0, unroll=8)


def _permute_rows(body, idx, src, n_dst, name):
    m = idx.shape[0]
    width = src.shape[1]
    rows = _pick(m, (512, 256, 128))
    tile = pl.BlockSpec((rows, width), lambda i, idx_ref: (i, 0))
    whole = pl.BlockSpec(memory_space=pl.ANY)
    scatter = body is _scatter_rows_kernel
    return pl.pallas_call(
        functools.partial(body, rows=rows),
        grid_spec=pltpu.PrefetchScalarGridSpec(
            num_scalar_prefetch=1,
            grid=(m // rows,),
            in_specs=[tile if scatter else whole],
            out_specs=whole if scatter else tile,
            scratch_shapes=[pltpu.SemaphoreType.DMA(())]),
        out_shape=jax.ShapeDtypeStruct((n_dst, width), src.dtype),
        compiler_params=_cparams(("arbitrary",)),
        name=name,
    )(idx, src)


def _moe_kernel(blk_ref, e_lo_ref, e_hi_ref, new_lo_ref, new_hi_ref, nv_ref, x_ref,
                wg_lo, wg_hi, wu_lo, wu_hi, wd_lo, wd_hi, g_ref, b_ref, o_ref, wg_scr, wu_scr, wd_scr):
    t = pl.program_id(0)
    nv = nv_ref[t]

    for j, (flag_ref, wg, wu, wd) in enumerate(((new_lo_ref, wg_lo, wu_lo, wd_lo), (new_hi_ref, wg_hi, wu_hi, wd_hi))):
        @pl.when(flag_ref[t] > 0)
        def _():
            wg_scr[j] = wg[...].astype(BF16)
            wu_scr[j] = wu[...].astype(BF16)
            wd_scr[j] = wd[...].astype(BF16)

    @pl.when(nv > 0)
    def _():
        tm, d = o_ref.shape
        live = lax.broadcasted_iota(jnp.int32, (tm, 1), 0) < nv
        x = jnp.where(live, x_ref[:, 0:d], 0.0)
        cw = jnp.where(live, x_ref[:, d:], 0.0)
        xb = x.astype(BF16)
        acc = ALPHA * x
        for j in range(TOP_K):
            gate = jnp.dot(xb, wg_scr[j], preferred_element_type=F32)
            up = jnp.dot(xb, wu_scr[j], preferred_element_type=F32)
            act = gate * jax.nn.sigmoid(gate) * up * cw[:, j:j + 1]
            acc = acc + jnp.dot(act.astype(BF16), wd_scr[j], preferred_element_type=F32)
        o_ref[...] = _layer_norm(acc, g_ref[...], b_ref[...])


def _moe(x_sorted, tables, wg, wu, wd, layer, g, b, tm, name):
    rows, wa = x_sorted.shape
    d = wa - LANES
    n_tiles = tables[0].shape[0]
    rows_of = lambda t, blk, e_lo, e_hi, new_lo, new_hi, nv: (blk[t], 0)
    lo = lambda t, blk, e_lo, e_hi, new_lo, new_hi, nv: (layer, e_lo[t], 0, 0)
    hi = lambda t, blk, e_lo, e_hi, new_lo, new_hi, nv: (layer, e_hi[t], 0, 0)
    const = lambda t, blk, e_lo, e_hi, new_lo, new_hi, nv: (0, 0)
    w_in = lambda f: pl.BlockSpec((None, None, d, D_EXPERT), f)
    w_out = lambda f: pl.BlockSpec((None, None, D_EXPERT, d), f)
    return pl.pallas_call(
        _moe_kernel,
        grid_spec=pltpu.PrefetchScalarGridSpec(
            num_scalar_prefetch=6,
            grid=(n_tiles,),
            in_specs=[pl.BlockSpec((tm, wa), rows_of),
                      w_in(lo), w_in(hi), w_in(lo), w_in(hi), w_out(lo), w_out(hi),
                      pl.BlockSpec((1, d), const), pl.BlockSpec((1, d), const)],
            out_specs=pl.BlockSpec((tm, d), rows_of),
            scratch_shapes=[pltpu.VMEM((TOP_K, d, D_EXPERT), BF16), pltpu.VMEM((TOP_K, d, D_EXPERT), BF16),
                            pltpu.VMEM((TOP_K, D_EXPERT, d), BF16)]),
        out_shape=jax.ShapeDtypeStruct((rows, d), F32),
        compiler_params=_cparams(("arbitrary",)),
        name=name,
    )(*tables, x_sorted, wg, wg, wu, wu, wd, wd, g, b)


def _bucket_layout(counts, tm, n_tiles):
    tiles_b = (counts + tm - 1) // tm
    ends = jnp.cumsum(tiles_b)
    starts = ends - tiles_b
    t = jnp.arange(n_tiles, dtype=jnp.int32)
    tc = jnp.minimum(t, ends[-1] - 1)
    bkt = jnp.sum((tc[:, None] >= ends[None, :]).astype(jnp.int32), axis=1)
    nv = jnp.where(t < ends[-1], jnp.clip(counts[bkt] - (tc - starts[bkt]) * tm, 0, tm), 0)
    pair_lo = jnp.array([p[0] for p in PAIRS], jnp.int32)
    pair_hi = jnp.array([p[1] for p in PAIRS], jnp.int32)
    grp, pair = bkt // len(PAIRS), bkt % len(PAIRS)
    e_lo = grp * EXPERTS_PER_GROUP + pair_lo[pair]
    e_hi = grp * EXPERTS_PER_GROUP + pair_hi[pair]
    changed = lambda e: jnp.concatenate([jnp.ones((1,), jnp.int32), (e[1:] != e[:-1]).astype(jnp.int32)])
    i32 = lambda a: a.astype(jnp.int32)
    return i32(starts * tm), (i32(tc), i32(e_lo), i32(e_hi), changed(e_lo), changed(e_hi), i32(nv))


def _routed_moe(routed, wg, wu, wd, layer, g, b, out_idx_fn, n_out, name):
    haug, bucket, rank, cnt = routed
    n = haug.shape[0]
    tm = _pick(n, (256, 128))
    n_tiles = n // tm + N_BUCKETS
    counts = cnt[:N_BUCKETS, 0].astype(jnp.int32)
    row0, tables = _bucket_layout(counts, tm, n_tiles)
    pos = row0[bucket.reshape(n)] + rank.reshape(n)
    x_sorted = _permute_rows(_scatter_rows_kernel, pos, haug, n_tiles * tm, name + "_scatter")
    y_sorted = _moe(x_sorted, tables, wg, wu, wd, layer, g, b, tm, name)
    return _permute_rows(_gather_rows_kernel, out_idx_fn(pos), y_sorted, n_out, name + "_gather")


def kernel(x, meta, even_w_in, pool_w, pool_scale, diff_lq1, diff_lk1, diff_lq2, diff_lk2, diff_subln,
           even_w_out, odd_w_in, gla_gate_w2, gla_gate_b, gla_head_norm, odd_w_out, ln_mix_g, ln_mix_b,
           ln_ffn_g, ln_ffn_b, router_w, router_bias, moe_w_gate, moe_w_up, moe_w_down):
    bsz, seq, d = x.shape
    l = PREFIX + seq
    n = bsz * l
    tm = 5 * PREFIX
    assert l % tm == 0
    prefix = jnp.concatenate([jnp.zeros((FIRST_VALID, d), x.dtype), meta.astype(x.dtype)], axis=0)
    h = (x, prefix)
    router_w_t = router_w.T
    row = lambda a: a.reshape(1, -1)

    for i in range(DEPTH):
        j = i // 2
        if i % 2 == 0:
            lambda_init = 0.8 - 0.6 * math.exp(-0.3 * i)
            w_in = even_w_in[j].astype(BF16)
            z = _proj_stream(*h, w_in, tm, "even_in_proj") if isinstance(h, tuple) else _proj(h, w_in, "even_in_proj")
            z = z.reshape(bsz, l, EVEN_IN)
            y_pool = _pool(z, pool_w[j].astype(BF16), row(pool_scale[j]))
            lam_params = jnp.stack([diff_lq1[j], diff_lk1[j], diff_lq2[j], diff_lk2[j]])
            y_attn = _attn(z, lam_params, diff_subln[j].reshape(-1, 1), lambda_init)
            ys = [y_pool.reshape(n, POOL_WIDTH), y_attn.reshape(n, DIFF_WIDTH)]
            routed = _mixout_route(ys, even_w_out[j].astype(BF16), h, row(ln_mix_g[i]), row(ln_mix_b[i]),
                                   router_w_t, router_bias, tm, "even_out_proj")
        else:
            w_in = jnp.pad(odd_w_in[j], ((0, 0), (0, ODD_IN_PAD - ODD_IN))).astype(BF16)
            z = _proj(h, w_in, "odd_in_proj").reshape(bsz, l, ODD_IN_PAD)
            w2 = jnp.pad(gla_gate_w2[j], ((0, LANES - GLA_RANK), (0, 0))).astype(BF16)
            y = _gla(z, w2, row(gla_gate_b[j]), row(gla_head_norm[j]))
            routed = _mixout_route([y.reshape(n, GLA_V_WIDTH)], odd_w_out[j].astype(BF16), h,
                                   row(ln_mix_g[i]), row(ln_mix_b[i]), router_w_t, router_bias, tm, "odd_out_proj")
        last = i == DEPTH - 1
        if last:
            out_idx_fn = lambda pos: pos.reshape(bsz, l)[:, PREFIX:].reshape(-1)
        else:
            out_idx_fn = lambda pos: pos
        h = _routed_moe(routed, moe_w_gate, moe_w_up, moe_w_down, i, row(ln_ffn_g[i]), row(ln_ffn_b[i]),
                        out_idx_fn, bsz * seq if last else n, f"moe_{i}")
    return h.reshape(bsz, seq, d)
```
